```python
import math
import jax, jax.numpy as jnp
from jax import lax
import numpy as np

D_MODEL = 1024
BATCH = 32
SEQ = 2048
DEPTH = 4
DEC_BATCH = 16
DEC_SEQ = 4096
PAST_LEN = 128

RMS_EPS = 1e-6
ROPE_THETA = 10000.0
NEG_INF = -1e30
N_EVEN = (DEPTH + 1) // 2
N_ODD = DEPTH // 2
ATT_HEAD_DIM = 64
MLA_HEADS = 8
MLA_NOPE = 64
MLA_ROPE = 32
MLA_QK = MLA_NOPE + MLA_ROPE
MLA_V = 64
MLA_Q_RANK = 384
MLA_KV_RANK = 256
MLA_Q_BLOCK = 128
DIL_HEADS = 8
DIL_PATTERNS = ((128, 1), (512, 4), (2048, 16))
DIL_BLOCK = 64
WIN_Q_HEADS = 16
WIN_KV_HEADS = 4
WIN_GROUP = WIN_Q_HEADS // WIN_KV_HEADS
WIN_HALF = 128
WIN_BLOCK = 128
EVEN_IN = MLA_Q_RANK + MLA_KV_RANK + MLA_ROPE + 3 * DIL_HEADS * ATT_HEAD_DIM
EVEN_MIX = MLA_HEADS * MLA_V + DIL_HEADS * ATT_HEAD_DIM
ODD_IN = (WIN_Q_HEADS + 2 * WIN_KV_HEADS) * ATT_HEAD_DIM
ODD_MIX = WIN_Q_HEADS * ATT_HEAD_DIM
FF_DENSE = 2816
N_EXPERTS = 8
TOP_K = 2
FF_EXPERT = 3584

kernel_name = 'hybrid_mla_dilated_window_encoder'


def rms_norm(x, g):
    xf = x.astype(jnp.float32)
    y = xf * lax.rsqrt(jnp.mean(xf * xf, axis=-1, keepdims=True) + RMS_EPS)
    return (y * g.astype(jnp.float32)).astype(x.dtype)


def rope_tables(T, dim):
    inv = ROPE_THETA ** (-jnp.arange(0, dim, 2, dtype=jnp.float32) / dim)
    ang = jnp.arange(T, dtype=jnp.float32)[:, None] * inv[None, :]
    return jnp.cos(ang), jnp.sin(ang)


def apply_rope(x, cos, sin):
    x1, x2 = jnp.split(x, 2, axis=-1)
    c = cos[:, None, :]
    s = sin[:, None, :]
    return jnp.concatenate([x1 * c - x2 * s, x1 * s + x2 * c], axis=-1).astype(x.dtype)


def swiglu(h, w_up, w_down):
    g, u = jnp.split(h @ w_up, 2, axis=-1)
    return (jax.nn.silu(g) * u) @ w_down


def banded_attention(q, k, v, half, block, scale):
    N, L, Hk, G, dh = q.shape
    bq = min(block, L)
    nb = -(-L // bq)
    Lp = nb * bq
    pad = Lp - L
    span = bq + 2 * half
    qp = jnp.pad(q, ((0, 0), (0, pad), (0, 0), (0, 0), (0, 0)))
    kp = jnp.pad(k, ((0, 0), (half, half + pad), (0, 0), (0, 0)))
    vp = jnp.pad(v, ((0, 0), (half, half + pad), (0, 0), (0, 0)))
    qb = jnp.moveaxis(qp.reshape(N, nb, bq, Hk, G, dh), 1, 0)
    starts = jnp.arange(nb, dtype=jnp.int32) * bq

    def one_block(args):
        qblk, start = args
        kb = lax.dynamic_slice_in_dim(kp, start, span, axis=1)
        vb = lax.dynamic_slice_in_dim(vp, start, span, axis=1)
        s = jnp.einsum('nqhgd,nkhd->nhgqk', qblk, kb, preferred_element_type=jnp.float32) * scale
        kpos = start - half + jnp.arange(span)
        qpos = start + jnp.arange(bq)
        diff = kpos[None, :] - qpos[:, None]
        mask = (jnp.abs(diff) <= half) & (kpos >= 0)[None, :] & (kpos < L)[None, :]
        s = jnp.where(mask, s, NEG_INF)
        lse = jax.nn.logsumexp(s, axis=-1)
        p = jnp.exp(s - lse[..., None])
        o = jnp.einsum('nhgqk,nkhd->nqhgd', p.astype(vb.dtype), vb)
        return o, jnp.moveaxis(lse, -1, 1)

    o, lse = lax.map(one_block, (qb, starts))
    o = jnp.moveaxis(o, 0, 1).reshape(N, Lp, Hk, G, dh)[:, :L]
    lse = jnp.moveaxis(lse, 0, 1).reshape(N, Lp, Hk, G)[:, :L]
    return o, lse


def dense_attention(q, k, v, scale):
    B, T, H, dq = q.shape
    nb = T // MLA_Q_BLOCK
    qb = jnp.moveaxis(q.reshape(B, nb, MLA_Q_BLOCK, H, dq), 1, 0)

    def one_block(qblk):
        s = jnp.einsum('bqhd,bkhd->bhqk', qblk, k, preferred_element_type=jnp.float32) * scale
        p = jax.nn.softmax(s, axis=-1)
        return jnp.einsum('bhqk,bkhd->bqhd', p.astype(v.dtype), v)

    o = lax.map(one_block, qb)
    return jnp.moveaxis(o, 0, 1).reshape(B, T, H, v.shape[-1])


def to_strided(t, dil):
    B, T, H, dh = t.shape
    return t.reshape(B, T // dil, dil, H, dh).transpose(0, 2, 1, 3, 4).reshape(B * dil, T // dil, H, dh)


def dilated_attention(q, k, v):
    B, T, H, dh = q.shape
    outs, lses = [], []
    for window, dil in DIL_PATTERNS:
        half = window // (2 * dil)
        o, lse = banded_attention(to_strided(q, dil)[:, :, :, None, :], to_strided(k, dil),
                                  to_strided(v, dil), half, DIL_BLOCK, dh ** -0.5)
        o = o[:, :, :, 0].reshape(B, dil, T // dil, H, dh).transpose(0, 2, 1, 3, 4).reshape(B, T, H, dh)
        lse = lse[..., 0].reshape(B, dil, T // dil, H).transpose(0, 2, 1, 3).reshape(B, T, H)
        outs.append(o)
        lses.append(lse)
    wts = jax.nn.softmax(jnp.stack(lses, axis=0), axis=0)
    return jnp.einsum('pbth,pbthd->bthd', wts.astype(q.dtype), jnp.stack(outs, axis=0))


def mla_dilated_mixer(h, cos_r, sin_r, cos_h, sin_h, w_in, q_latent_g, q_up, kv_latent_g, kv_up,
                      mla_qk_g, dil_qk_g, w_out):
    B, T, _ = h.shape
    proj = h @ w_in
    c_q, c_kv, k_rope, qkv_b = jnp.split(
        proj, [MLA_Q_RANK, MLA_Q_RANK + MLA_KV_RANK, MLA_Q_RANK + MLA_KV_RANK + MLA_ROPE], axis=-1)
    q_a = (rms_norm(c_q, q_latent_g) @ q_up).reshape(B, T, MLA_HEADS, MLA_QK)
    kv_a = (rms_norm(c_kv, kv_latent_g) @ kv_up).reshape(B, T, MLA_HEADS, MLA_NOPE + MLA_V)
    k_nope, v_a = jnp.split(kv_a, [MLA_NOPE], axis=-1)
    k_r = jnp.broadcast_to(k_rope[:, :, None, :], (B, T, MLA_HEADS, MLA_ROPE))
    k_a = jnp.concatenate([k_nope, k_r], axis=-1)
    q_a = rms_norm(q_a, mla_qk_g[0])
    k_a = rms_norm(k_a, mla_qk_g[1])
    q_a = jnp.concatenate([q_a[..., :MLA_NOPE], apply_rope(q_a[..., MLA_NOPE:], cos_r, sin_r)], axis=-1)
    k_a = jnp.concatenate([k_a[..., :MLA_NOPE], apply_rope(k_a[..., MLA_NOPE:], cos_r, sin_r)], axis=-1)
    o_a = dense_attention(q_a, k_a, v_a, MLA_QK ** -0.5)
    qkv = qkv_b.reshape(B, T, 3, DIL_HEADS, ATT_HEAD_DIM)
    q_b = apply_rope(rms_norm(qkv[:, :, 0], dil_qk_g[0]), cos_h, sin_h)
    k_b = apply_rope(rms_norm(qkv[:, :, 1], dil_qk_g[1]), cos_h, sin_h)
    o_b = dilated_attention(q_b, k_b, qkv[:, :, 2])
    o = jnp.concatenate([o_a.reshape(B, T, -1), o_b.reshape(B, T, -1)], axis=-1)
    return o @ w_out


def window_gqa_mixer(h, cos_h, sin_h, w_in, qk_g, sink, w_out):
    B, T, _ = h.shape
    q, k, v = jnp.split(h @ w_in, [WIN_Q_HEADS * ATT_HEAD_DIM,
                                   (WIN_Q_HEADS + WIN_KV_HEADS) * ATT_HEAD_DIM], axis=-1)
    q = apply_rope(rms_norm(q.reshape(B, T, WIN_Q_HEADS, ATT_HEAD_DIM), qk_g[0]), cos_h, sin_h)
    k = apply_rope(rms_norm(k.reshape(B, T, WIN_KV_HEADS, ATT_HEAD_DIM), qk_g[1]), cos_h, sin_h)
    v = v.reshape(B, T, WIN_KV_HEADS, ATT_HEAD_DIM)
    q = q.reshape(B, T, WIN_KV_HEADS, WIN_GROUP, ATT_HEAD_DIM)
    o, lse = banded_attention(q, k, v, WIN_HALF, WIN_BLOCK, ATT_HEAD_DIM ** -0.5)
    keep = jax.nn.sigmoid(lse - sink.reshape(WIN_KV_HEADS, WIN_GROUP).astype(jnp.float32))
    o = o * keep[..., None].astype(o.dtype)
    return o.reshape(B, T, ODD_MIX) @ w_out


def moe_swiglu(h, router_w, w_up, w_down):
    logits = jnp.einsum('btd,de->bte', h, router_w, preferred_element_type=jnp.float32)
    probs = jax.nn.softmax(logits, axis=-1)
    top_p, top_i = lax.top_k(probs, TOP_K)
    top_p = top_p / jnp.sum(top_p, axis=-1, keepdims=True)
    gates = jnp.sum(jax.nn.one_hot(top_i, N_EXPERTS, dtype=jnp.float32) * top_p[..., None], axis=-2)
    out = jnp.zeros_like(h)
    for e in range(N_EXPERTS):
        out = out + gates[..., e:e + 1].astype(h.dtype) * swiglu(h, w_up[e], w_down[e])
    return out


def trunk(x, c, ada_w, ada_b, norm_g, even_w_in, even_q_latent_g, even_q_up, even_kv_latent_g,
          even_kv_up, even_mla_qk_g, even_dil_qk_g, even_w_out, odd_w_in, odd_qk_g, odd_sink,
          odd_w_out, dense_w_up, dense_w_down, moe_router, moe_w_up, moe_w_down):
    T = x.shape[1]
    cos_r, sin_r = rope_tables(T, MLA_ROPE)
    cos_h, sin_h = rope_tables(T, ATT_HEAD_DIM)
    cond = jax.nn.silu(c)
    for l in range(DEPTH):
        i = l // 2
        mod = (cond @ ada_w[l] + ada_b[l])[:, None, :]
        sh1, sc1, g1, sh2, sc2, g2 = jnp.split(mod, 6, axis=-1)
        h = rms_norm(x, norm_g[l, 0]) * (1 + sc1) + sh1
        if l % 2 == 0:
            mix = mla_dilated_mixer(h, cos_r, sin_r, cos_h, sin_h, even_w_in[i], even_q_latent_g[i],
                                    even_q_up[i], even_kv_latent_g[i], even_kv_up[i],
                                    even_mla_qk_g[i], even_dil_qk_g[i], even_w_out[i])
        else:
            mix = window_gqa_mixer(h, cos_h, sin_h, odd_w_in[i], odd_qk_g[i], odd_sink[i], odd_w_out[i])
        x = x + g1 * mix
        h = rms_norm(x, norm_g[l, 1]) * (1 + sc2) + sh2
        if l % 2 == 0:
            ff = swiglu(h, dense_w_up[i], dense_w_down[i])
        else:
            ff = moe_swiglu(h, moe_router[i], moe_w_up[i], moe_w_down[i])
        x = x + g2 * ff
    return x


def setup_inputs(seed: int = 0) -> dict:
    key = jax.random.key(seed)
    ks = jax.random.split(key, 24)

    def nrm(k, shape):
        return jax.random.normal(k, shape, jnp.float32)

    def w(k, shape, fan_in):
        return nrm(k, shape) * fan_in ** -0.5

    def gain(k, shape):
        return 1.0 + 0.05 * nrm(k, shape)

    return {
        'x_prompt': nrm(ks[0], (BATCH, SEQ, D_MODEL)),
        'x_sample': nrm(ks[1], (DEC_BATCH, DEC_SEQ, D_MODEL)),
        'c_prompt': nrm(ks[2], (BATCH, D_MODEL)),
        'c_sample': nrm(ks[3], (DEC_BATCH, D_MODEL)),
        'ada_w': 0.5 * w(ks[4], (DEPTH, D_MODEL, 6 * D_MODEL), D_MODEL),
        'ada_b': 0.02 * nrm(ks[5], (DEPTH, 6 * D_MODEL)),
        'norm_g': gain(ks[6], (DEPTH, 2, D_MODEL)),
        'even_w_in': w(ks[7], (N_EVEN, D_MODEL, EVEN_IN), D_MODEL),
        'even_q_latent_g': gain(ks[8], (N_EVEN, MLA_Q_RANK)),
        'even_q_up': w(ks[9], (N_EVEN, MLA_Q_RANK, MLA_HEADS * MLA_QK), MLA_Q_RANK),
        'even_kv_latent_g': gain(ks[10], (N_EVEN, MLA_KV_RANK)),
        'even_kv_up': w(ks[11], (N_EVEN, MLA_KV_RANK, MLA_HEADS * (MLA_NOPE + MLA_V)), MLA_KV_RANK),
        'even_mla_qk_g': gain(ks[12], (N_EVEN, 2, MLA_QK)),
        'even_dil_qk_g': gain(ks[13], (N_EVEN, 2, ATT_HEAD_DIM)),
        'even_w_out': w(ks[14], (N_EVEN, EVEN_MIX, D_MODEL), EVEN_MIX),
        'odd_w_in': w(ks[15], (N_ODD, D_MODEL, ODD_IN), D_MODEL),
        'odd_qk_g': gain(ks[16], (N_ODD, 2, ATT_HEAD_DIM)),
        'odd_sink': 3.0 + 0.5 * nrm(ks[17], (N_ODD, WIN_Q_HEADS)),
        'odd_w_out': w(ks[18], (N_ODD, ODD_MIX, D_MODEL), ODD_MIX),
        'dense_w_up': w(ks[19], (N_EVEN, D_MODEL, 2 * FF_DENSE), D_MODEL),
        'dense_w_down': w(ks[20], (N_EVEN, FF_DENSE, D_MODEL), FF_DENSE),
        'moe_router': w(ks[21], (N_ODD, D_MODEL, N_EXPERTS), D_MODEL),
        'moe_w_up': w(ks[22], (N_ODD, N_EXPERTS, D_MODEL, 2 * FF_EXPERT), D_MODEL),
        'moe_w_down': w(ks[23], (N_ODD, N_EXPERTS, FF_EXPERT, D_MODEL), FF_EXPERT),
    }


def reference(x_prompt, x_sample, c_prompt, c_sample, ada_w, ada_b, norm_g, even_w_in,
              even_q_latent_g, even_q_up, even_kv_latent_g, even_kv_up, even_mla_qk_g,
              even_dil_qk_g, even_w_out, odd_w_in, odd_qk_g, odd_sink, odd_w_out, dense_w_up,
              dense_w_down, moe_router, moe_w_up, moe_w_down):
    y_prompt = trunk(x_prompt, c_prompt, ada_w, ada_b, norm_g, even_w_in, even_q_latent_g, even_q_up,
                     even_kv_latent_g, even_kv_up, even_mla_qk_g, even_dil_qk_g, even_w_out, odd_w_in,
                     odd_qk_g, odd_sink, odd_w_out, dense_w_up, dense_w_down, moe_router, moe_w_up,
                     moe_w_down)
    y_sample = trunk(x_sample, c_sample, ada_w, ada_b, norm_g, even_w_in, even_q_latent_g, even_q_up,
                     even_kv_latent_g, even_kv_up, even_mla_qk_g, even_dil_qk_g, even_w_out, odd_w_in,
                     odd_qk_g, odd_sink, odd_w_out, dense_w_up, dense_w_down, moe_router, moe_w_up,
                     moe_w_down)
    return (y_prompt, y_sample)
```

```python
import functools
import math

import jax
import jax.numpy as jnp
from jax import lax
from jax.experimental import pallas as pl
from jax.experimental.pallas import tpu as pltpu

D_MODEL = 1024
DEPTH = 4
RMS_EPS = 1e-6
ROPE_THETA = 10000.0
NEG_INF = -1e30
HEAD_DIM = 64
MLA_HEADS = 8
MLA_NOPE = 64
MLA_ROPE = 32
MLA_QK = MLA_NOPE + MLA_ROPE
MLA_V = 64
MLA_Q_RANK = 384
MLA_KV_RANK = 256
DIL_HEADS = 8
DIL_PATTERNS = ((128, 1), (512, 4), (2048, 16))
WIN_Q_HEADS = 16
WIN_KV_HEADS = 4
WIN_HALF = 128
FF_DENSE = 2816
N_EXPERTS = 8
FF_EXPERT = 3584

LANES = 128
EVEN_PROJ = MLA_Q_RANK + MLA_KV_RANK + LANES + 3 * DIL_HEADS * HEAD_DIM
KR_OFF = MLA_Q_RANK + MLA_KV_RANK
DIL_OFF = KR_OFF + LANES
ODD_PROJ = WIN_Q_HEADS * HEAD_DIM + 2 * WIN_KV_HEADS * LANES

ROW_TILE = 512
MLA_Q_TILE = 256
BAND_Q_TILE = 128
MOE_ROW_TILE = 1024
MOE_FF_CHUNK = 512
DENSE_FF_CHUNK = 1408
VMEM_LIMIT = 56 * 1024 * 1024

BF16 = jnp.bfloat16
F32 = jnp.float32


def _cparams(*sem):
    return pltpu.CompilerParams(dimension_semantics=sem, vmem_limit_bytes=VMEM_LIMIT)


def _const_spec(shape):
    nd = len(shape)
    return pl.BlockSpec(shape, lambda *_: (0,) * nd, pipeline_mode=pl.Buffered(1))


def _dot(a, b):
    return jnp.dot(a, b, preferred_element_type=F32)


def _dot_t(a, b):
    return lax.dot_general(a, b, (((1,), (1,)), ((), ())), preferred_element_type=F32)


def _split_bf16(a):
    hi = a.astype(BF16)
    lo = (a - hi.astype(F32)).astype(BF16)
    return hi, lo


def _dot3(a, b):
    ah, al = _split_bf16(a)
    bh, bl = _split_bf16(b)
    return _dot(ah, bh) + (_dot(ah, bl) + _dot(al, bh))


def _sigmoid(x):
    return 1.0 / (1.0 + jnp.exp(-x))


def _norm_mod(x, g, sc, sh):
    ms = jnp.mean(x * x, axis=-1, keepdims=True)
    return (x * lax.rsqrt(ms + RMS_EPS) * g) * (1.0 + sc) + sh


def _rms(x, g):
    ms = jnp.mean(x * x, axis=-1, keepdims=True)
    return x * lax.rsqrt(ms + RMS_EPS) * g


def _lane_lo(shape=(1, LANES)):
    return lax.broadcasted_iota(jnp.int32, shape, len(shape) - 1) < HEAD_DIM


def _rope_slab(slab, tab, shift):
    return (slab * tab[0] + pltpu.roll(slab, shift, 1) * tab[1]
            + pltpu.roll(slab, LANES - shift, 1) * tab[2])


def _norm_rope_pair(slab, tab):
    lo = _lane_lo()
    sq = slab * slab
    ss_all = jnp.sum(sq, axis=-1, keepdims=True)
    ss_lo = jnp.sum(jnp.where(lo, sq, 0.0), axis=-1, keepdims=True)
    r_lo = lax.rsqrt(ss_lo * (1.0 / HEAD_DIM) + RMS_EPS)
    r_hi = lax.rsqrt((ss_all - ss_lo) * (1.0 / HEAD_DIM) + RMS_EPS)
    return _rope_slab(slab, tab, HEAD_DIM // 2) * jnp.where(lo, r_lo, r_hi)


def _ada_kernel(c_ref, w_ref, b_ref, o_ref):
    c = c_ref[...]
    o_ref[0] = _dot3(c * _sigmoid(c), w_ref[0]) + b_ref[0]


def _ada_mod(c, ada_w, ada_b):
    bc = c.shape[0]
    nchunk = ada_w.shape[2] // D_MODEL
    out = pl.pallas_call(
        _ada_kernel,
        grid=(DEPTH, nchunk),
        in_specs=[
            pl.BlockSpec((bc, D_MODEL), lambda l, j: (0, 0)),
            pl.BlockSpec((1, D_MODEL, D_MODEL), lambda l, j: (l, 0, j)),
            pl.BlockSpec((1, 1, D_MODEL), lambda l, j: (l, 0, j)),
        ],
        out_specs=pl.BlockSpec((1, bc, D_MODEL), lambda l, j: (l, 0, j)),
        out_shape=jax.ShapeDtypeStruct((DEPTH, bc, nchunk * D_MODEL), F32),
        compiler_params=_cparams("parallel", "parallel"),
        name="ada_mod",
    )(c, ada_w, ada_b.reshape(DEPTH, 1, -1))
    return out.reshape(DEPTH, bc, nchunk, D_MODEL)


def _even_in_kernel(x_ref, mod_ref, ng_ref, w_in_ref, qlg_ref, q_up_ref, kvlg_ref, kv_up_ref,
                    gkn_ref, tq_ref, tk_ref, tqb_ref, tkb_ref,
                    qa_ref, ka_ref, va_ref, qb_ref, kb_ref, vb_ref):
    h = _norm_mod(x_ref[0], ng_ref[...], mod_ref[0, 1:2, :], mod_ref[0, 0:1, :]).astype(BF16)
    proj = _dot(h, w_in_ref[...])

    cq = _rms(proj[:, :MLA_Q_RANK], qlg_ref[...]).astype(BF16)
    qraw = _dot(cq, q_up_ref[...])
    tq = (tq_ref[0], tq_ref[1], tq_ref[2])
    for hd in range(MLA_HEADS):
        slab = qraw[:, hd * LANES:(hd + 1) * LANES]
        ss = jnp.sum(slab * slab, axis=-1, keepdims=True)
        rinv = lax.rsqrt(ss * (1.0 / MLA_QK) + RMS_EPS)
        qa_ref[0, :, hd * LANES:(hd + 1) * LANES] = (
            _rope_slab(slab, tq, MLA_ROPE // 2) * rinv).astype(BF16)

    ckv = _rms(proj[:, MLA_Q_RANK:KR_OFF], kvlg_ref[...]).astype(BF16)
    kvraw = _dot(ckv, kv_up_ref[...])
    kr = proj[:, KR_OFF:DIL_OFF]
    ss_kr = jnp.sum(kr * kr, axis=-1, keepdims=True)
    kr_rot = _rope_slab(kr, (tk_ref[0], tk_ref[1], tk_ref[2]), MLA_ROPE // 2)
    gkn = gkn_ref[...]
    for hd in range(MLA_HEADS):
        slab = kvraw[:, hd * LANES:(hd + 1) * LANES]
        ss = jnp.sum(slab * slab, axis=-1, keepdims=True) + ss_kr
        rinv = lax.rsqrt(ss * (1.0 / MLA_QK) + RMS_EPS)
        ka_ref[0, :, hd * LANES:(hd + 1) * LANES] = ((slab * gkn + kr_rot) * rinv).astype(BF16)
    va_ref[0] = kvraw[:, MLA_HEADS * LANES:].astype(BF16)

    tqb = (tqb_ref[0], tqb_ref[1], tqb_ref[2])
    tkb = (tkb_ref[0], tkb_ref[1], tkb_ref[2])
    nsl = DIL_HEADS * HEAD_DIM // LANES
    for j in range(nsl):
        qs = proj[:, DIL_OFF + j * LANES:DIL_OFF + (j + 1) * LANES]
        qb_ref[0, :, j * LANES:(j + 1) * LANES] = _norm_rope_pair(qs, tqb).astype(BF16)
        ks = proj[:, DIL_OFF + (nsl + j) * LANES:DIL_OFF + (nsl + j + 1) * LANES]
        kb_ref[0, :, j * LANES:(j + 1) * LANES] = _norm_rope_pair(ks, tkb).astype(BF16)
    vb_ref[0] = proj[:, DIL_OFF + 2 * nsl * LANES:].astype(BF16)


def _even_in(x, mod_l, ng, wts, tabs):
    b, t, _ = x.shape
    tm = min(ROW_TILE, t)
    w_in, qlg, q_up, kvlg, kv_up, gkn = wts
    row = lambda c: pl.BlockSpec((1, tm, c), lambda i, j: (i, j, 0))
    tab = pl.BlockSpec((3, tm, LANES), lambda i, j: (0, j, 0))
    hc = DIL_HEADS * HEAD_DIM
    outs = [jax.ShapeDtypeStruct((b, t, c), BF16)
            for c in (MLA_HEADS * LANES, MLA_HEADS * LANES, MLA_HEADS * MLA_V, hc, hc, hc)]
    return pl.pallas_call(
        _even_in_kernel,
        grid=(b, t // tm),
        in_specs=[
            row(D_MODEL),
            pl.BlockSpec((1, 6, D_MODEL), lambda i, j: (i, 0, 0)),
            _const_spec(ng.shape), _const_spec(w_in.shape), _const_spec(qlg.shape),
            _const_spec(q_up.shape), _const_spec(kvlg.shape), _const_spec(kv_up.shape),
            _const_spec(gkn.shape), tab, tab, tab, tab,
        ],
        out_specs=[row(s.shape[2]) for s in outs],
        out_shape=outs,
        compiler_params=_cparams("parallel", "parallel"),
        name="even_in",
    )(x, mod_l, ng, w_in, qlg, q_up, kvlg, kv_up, gkn, *tabs)


def _mla_kernel(q_ref, k_ref, v_ref, o_ref):
    v = v_ref[0]
    outs = []
    for hd in range(2):
        q = q_ref[0, :, hd * LANES:(hd + 1) * LANES]
        k = k_ref[0, :, hd * LANES:(hd + 1) * LANES]
        s = _dot_t(q, k)
        m = jnp.max(s, axis=-1, keepdims=True)
        p = jnp.exp(s - m)
        l = jnp.sum(p, axis=-1, keepdims=True)
        outs.append(_dot(p.astype(BF16), v) * (1.0 / l))
    o_ref[0] = jnp.where(_lane_lo(), outs[0], outs[1]).astype(BF16)


def _mla_attention(q, k, v):
    b, t, _ = q.shape
    tq = min(MLA_Q_TILE, t)
    return pl.pallas_call(
        _mla_kernel,
        grid=(b, MLA_HEADS // 2, t // tq),
        in_specs=[
            pl.BlockSpec((1, tq, 2 * LANES), lambda i, c, j: (i, j, c)),
            pl.BlockSpec((1, t, 2 * LANES), lambda i, c, j: (i, 0, c)),
            pl.BlockSpec((1, t, LANES), lambda i, c, j: (i, 0, c)),
        ],
        out_specs=pl.BlockSpec((1, tq, LANES), lambda i, c, j: (i, j, c)),
        out_shape=jax.ShapeDtypeStruct((b, t, MLA_HEADS * MLA_V), BF16),
        compiler_params=_cparams("parallel", "parallel", "parallel"),
        name="mla_attn",
    )(q, k, v)


def _band_kernel(*refs, seq, tq, win, half, nq, ncb, with_lse, with_sink):
    it = iter(refs)
    q_ref, k_ref, v_ref = next(it), next(it), next(it)
    sink_ref = next(it) if with_sink else None
    o_ref = next(it)
    lse_ref = next(it) if with_lse else None

    lo = _lane_lo()
    m_lo = jnp.where(lo, 1.0, 0.0).astype(BF16)
    m_hi = jnp.where(lo, 0.0, 1.0).astype(BF16)
    rows = 2 * nq * tq
    rel = (lax.rem(lax.broadcasted_iota(jnp.int32, (rows, win), 0), tq)
           - lax.broadcasted_iota(jnp.int32, (rows, win), 1))

    def tile(ti, carry):
        q0 = pl.multiple_of(ti * tq, tq)
        if win == seq:
            ws = 0
        else:
            ws = pl.multiple_of(jnp.clip(q0 - half, 0, seq - win), HEAD_DIM)
        valid = jnp.abs(rel + (q0 - ws)) <= half
        for cb in range(ncb):
            kw = k_ref[0, pl.ds(ws, win), cb * LANES:(cb + 1) * LANES]
            vw = v_ref[0, pl.ds(ws, win), cb * LANES:(cb + 1) * LANES]
            parts = []
            for qi in range(nq):
                c0 = (cb * nq + qi) * LANES
                qs = q_ref[0, pl.ds(q0, tq), c0:c0 + LANES]
                parts += [qs * m_lo, qs * m_hi]
            s = _dot_t(jnp.concatenate(parts, axis=0), kw)
            s = jnp.where(valid, s, NEG_INF)
            m = jnp.max(s, axis=-1, keepdims=True)
            p = jnp.exp(s - m)
            l = jnp.sum(p, axis=-1, keepdims=True)
            o = _dot(p.astype(BF16), vw) * (1.0 / l)
            lse = m + jnp.log(l)
            for qi in range(nq):
                c0 = (cb * nq + qi) * LANES
                a, b = 2 * qi * tq, (2 * qi + 1) * tq
                o_slab = jnp.where(lo, o[a:a + tq], o[b:b + tq])
                lse_slab = jnp.where(lo, lse[a:a + tq], lse[b:b + tq])
                if with_sink:
                    o_slab = o_slab * _sigmoid(lse_slab - sink_ref[:, c0:c0 + LANES])
                o_ref[0, pl.ds(q0, tq), c0:c0 + LANES] = o_slab.astype(BF16)
                if with_lse:
                    lse_ref[0, pl.ds(q0, tq), c0:c0 + LANES] = lse_slab
        return carry

    lax.fori_loop(0, seq // tq, tile, 0)


def _band_attention(q, k, v, half, nq, ncb, sink=None, with_lse=False):
    b, seq, kc = k.shape
    nblk = kc // (LANES * ncb)
    tq = min(BAND_Q_TILE, seq)
    win = min(tq + 2 * half, seq)
    qw, kw = ncb * nq * LANES, ncb * LANES
    in_specs = [
        pl.BlockSpec((1, seq, qw), lambda i, c: (i, 0, c)),
        pl.BlockSpec((1, seq, kw), lambda i, c: (i, 0, c)),
        pl.BlockSpec((1, seq, kw), lambda i, c: (i, 0, c)),
    ]
    args = [q, k, v]
    if sink is not None:
        in_specs.append(pl.BlockSpec((1, qw), lambda i, c: (0, c)))
        args.append(sink)
    out_specs = [pl.BlockSpec((1, seq, qw), lambda i, c: (i, 0, c))]
    out_shape = [jax.ShapeDtypeStruct(q.shape, BF16)]
    if with_lse:
        out_specs.append(pl.BlockSpec((1, seq, qw), lambda i, c: (i, 0, c)))
        out_shape.append(jax.ShapeDtypeStruct(q.shape, F32))
    kern = functools.partial(_band_kernel, seq=seq, tq=tq, win=win, half=half, nq=nq, ncb=ncb,
                             with_lse=with_lse, with_sink=sink is not None)
    return pl.pallas_call(
        kern,
        grid=(b, nblk),
        in_specs=in_specs,
        out_specs=out_specs,
        out_shape=out_shape,
        compiler_params=_cparams("parallel", "parallel"),
        name="band_attn",
    )(*args)


def _dilated_attention(qb, kb, vb):
    b, t, c = qb.shape
    outs, lses = [], []
    for window, dil in DIL_PATTERNS:
        seq = t // dil
        view = lambda a: a.reshape(b, seq, dil * c)
        ncb = max(1, min(4, 2048 // seq))
        o, lse = _band_attention(view(qb), view(kb), view(vb), window // (2 * dil), 1, ncb,
                                 with_lse=True)
        outs.append(o.reshape(b, t, c))
        lses.append(lse.reshape(b, t, c))
    return outs, lses


def _even_out_kernel(x_ref, mod_ref, oa_ref, o1_ref, o2_ref, o3_ref, l1_ref, l2_ref, l3_ref,
                     w_ref, y_ref):
    l1, l2, l3 = l1_ref[0], l2_ref[0], l3_ref[0]
    m = jnp.maximum(jnp.maximum(l1, l2), l3)
    e1, e2, e3 = jnp.exp(l1 - m), jnp.exp(l2 - m), jnp.exp(l3 - m)
    ob = (e1 * o1_ref[0].astype(F32) + e2 * o2_ref[0].astype(F32)
          + e3 * o3_ref[0].astype(F32)) * (1.0 / (e1 + e2 + e3))
    half = w_ref.shape[0] // 2
    mix = _dot(oa_ref[0], w_ref[:half, :]) + _dot(ob.astype(BF16), w_ref[half:, :])
    y_ref[0] = x_ref[0] + mod_ref[0, 2:3, :] * mix


def _even_out(x, mod_l, oa, outs, lses, w_out):
    b, t, _ = x.shape
    tm = min(ROW_TILE, t)
    row = lambda c: pl.BlockSpec((1, tm, c), lambda i, j: (i, j, 0))
    hc = oa.shape[2]
    return pl.pallas_call(
        _even_out_kernel,
        grid=(b, t // tm),
        in_specs=[row(D_MODEL), pl.BlockSpec((1, 6, D_MODEL), lambda i, j: (i, 0, 0))]
        + [row(hc)] * 7 + [_const_spec(w_out.shape)],
        out_specs=row(D_MODEL),
        out_shape=jax.ShapeDtypeStruct(x.shape, F32),
        compiler_params=_cparams("parallel", "parallel"),
        name="even_out",
    )(x, mod_l, oa, *outs, *lses, w_out)


def _dense_ffn_kernel(x_ref, mod_ref, ng_ref, wu_ref, wd_ref, y_ref):
    x = x_ref[0]
    h = _norm_mod(x, ng_ref[...], mod_ref[0, 4:5, :], mod_ref[0, 3:4, :]).astype(BF16)
    ff = wd_ref.shape[0]
    acc = None
    for c0 in range(0, ff, DENSE_FF_CHUNK):
        g = _dot(h, wu_ref[:, c0:c0 + DENSE_FF_CHUNK])
        u = _dot(h, wu_ref[:, ff + c0:ff + c0 + DENSE_FF_CHUNK])
        a = (g * _sigmoid(g) * u).astype(BF16)
        d = _dot(a, wd_ref[c0:c0 + DENSE_FF_CHUNK, :])
        acc = d if acc is None else acc + d
    y_ref[0] = x + mod_ref[0, 5:6, :] * acc


def _dense_ffn(x, mod_l, ng, w_up, w_down):
    b, t, _ = x.shape
    tm = min(ROW_TILE, t)
    row = pl.BlockSpec((1, tm, D_MODEL), lambda i, j: (i, j, 0))
    return pl.pallas_call(
        _dense_ffn_kernel,
        grid=(b, t // tm),
        in_specs=[row, pl.BlockSpec((1, 6, D_MODEL), lambda i, j: (i, 0, 0)),
                  _const_spec(ng.shape), _const_spec(w_up.shape), _const_spec(w_down.shape)],
        out_specs=row,
        out_shape=jax.ShapeDtypeStruct(x.shape, F32),
        compiler_params=_cparams("parallel", "parallel"),
        name="dense_ffn",
    )(x, mod_l, ng, w_up, w_down)


def _odd_in_kernel(x_ref, mod_ref, ng_ref, w_in_ref, tq_ref, tk_ref, q_ref, k_ref, v_ref):
    h = _norm_mod(x_ref[0], ng_ref[...], mod_ref[0, 1:2, :], mod_ref[0, 0:1, :]).astype(BF16)
    proj = _dot(h, w_in_ref[...])
    tq = (tq_ref[0], tq_ref[1], tq_ref[2])
    tk = (tk_ref[0], tk_ref[1], tk_ref[2])
    nq = WIN_Q_HEADS * HEAD_DIM // LANES
    for j in range(nq):
        q_ref[0, :, j * LANES:(j + 1) * LANES] = _norm_rope_pair(
            proj[:, j * LANES:(j + 1) * LANES], tq).astype(BF16)
    for j in range(WIN_KV_HEADS):
        c0 = (nq + j) * LANES
        k_ref[0, :, j * LANES:(j + 1) * LANES] = _norm_rope_pair(
            proj[:, c0:c0 + LANES], tk).astype(BF16)
    v_ref[0] = proj[:, (nq + WIN_KV_HEADS) * LANES:].astype(BF16)


def _odd_in(x, mod_l, ng, w_in, tabs):
    b, t, _ = x.shape
    tm = min(ROW_TILE, t)
    row = lambda c: pl.BlockSpec((1, tm, c), lambda i, j: (i, j, 0))
    tab = pl.BlockSpec((3, tm, LANES), lambda i, j: (0, j, 0))
    widths = (WIN_Q_HEADS * HEAD_DIM, WIN_KV_HEADS * LANES, WIN_KV_HEADS * LANES)
    return pl.pallas_call(
        _odd_in_kernel,
        grid=(b, t // tm),
        in_specs=[row(D_MODEL), pl.BlockSpec((1, 6, D_MODEL), lambda i, j: (i, 0, 0)),
                  _const_spec(ng.shape), _const_spec(w_in.shape), tab, tab],
        out_specs=[row(c) for c in widths],
        out_shape=[jax.ShapeDtypeStruct((b, t, c), BF16) for c in widths],
        compiler_params=_cparams("parallel", "parallel"),
        name="odd_in",
    )(x, mod_l, ng, w_in, *tabs)


def _odd_out_kernel(x_ref, mod_ref, o_ref, w_ref, y_ref):
    y_ref[0] = x_ref[0] + mod_ref[0, 2:3, :] * _dot(o_ref[0], w_ref[...])


def _odd_out(x, mod_l, o, w_out):
    b, t, _ = x.shape
    tm = min(ROW_TILE, t)
    row = pl.BlockSpec((1, tm, D_MODEL), lambda i, j: (i, j, 0))
    return pl.pallas_call(
        _odd_out_kernel,
        grid=(b, t // tm),
        in_specs=[row, pl.BlockSpec((1, 6, D_MODEL), lambda i, j: (i, 0, 0)), row,
                  _const_spec(w_out.shape)],
        out_specs=row,
        out_shape=jax.ShapeDtypeStruct(x.shape, F32),
        compiler_params=_cparams("parallel", "parallel"),
        name="odd_out",
    )(x, mod_l, o, w_out)


def _router_kernel(x_ref, mod_ref, ng_ref, wr_ref, h_ref, gate_ref, sel_ref):
    h = _norm_mod(x_ref[0], ng_ref[...], mod_ref[0, 4:5, :], mod_ref[0, 3:4, :])
    h_ref[0] = h
    lane = lax.broadcasted_iota(jnp.int32, (1, LANES), 1).astype(F32)
    logits = jnp.where(lane < N_EXPERTS, _dot3(h, wr_ref[...]), NEG_INF)
    e = jnp.exp(logits - jnp.max(logits, axis=-1, keepdims=True))
    p = e * (1.0 / jnp.sum(e, axis=-1, keepdims=True))
    p = jnp.where(lane < N_EXPERTS, p, -1.0)
    m1 = jnp.max(p, axis=-1, keepdims=True)
    i1 = jnp.min(jnp.where(p == m1, lane, float(LANES)), axis=-1, keepdims=True)
    s1 = lane == i1
    p2 = jnp.where(s1, -1.0, p)
    m2 = jnp.max(p2, axis=-1, keepdims=True)
    i2 = jnp.min(jnp.where(p2 == m2, lane, float(LANES)), axis=-1, keepdims=True)
    s2 = lane == i2
    inv = 1.0 / (m1 + m2)
    gate_ref[0] = jnp.where(s1, m1 * inv, 0.0) + jnp.where(s2, m2 * inv, 0.0)
    sel_ref[0] = jnp.where(s1, 1.0, 0.0) + jnp.where(s2, 1.0, 0.0)


def _router(x, mod_l, ng, wr):
    b, t, _ = x.shape
    tm = min(ROW_TILE, t)
    row = lambda c: pl.BlockSpec((1, tm, c), lambda i, j: (i, j, 0))
    return pl.pallas_call(
        _router_kernel,
        grid=(b, t // tm),
        in_specs=[row(D_MODEL), pl.BlockSpec((1, 6, D_MODEL), lambda i, j: (i, 0, 0)),
                  _const_spec(ng.shape), _const_spec(wr.shape)],
        out_specs=[row(D_MODEL), row(LANES), row(LANES)],
        out_shape=[jax.ShapeDtypeStruct(x.shape, F32),
                   jax.ShapeDtypeStruct((b, t, LANES), F32),
                   jax.ShapeDtypeStruct((b, t, LANES), F32)],
        compiler_params=_cparams("parallel", "parallel"),
        name="moe_router",
    )(x, mod_l, ng, wr)


GATHER_ROWS = 1024


def _gather_kernel(idx_ref, src_ref, dst_ref, sem):
    base = pl.program_id(0) * GATHER_ROWS

    def copy(r):
        return pltpu.make_async_copy(src_ref.at[pl.ds(idx_ref[r], 1), :],
                                     dst_ref.at[pl.ds(base + r, 1), :], sem)

    def start(r, c):
        copy(r).start()
        return c

    def wait(r, c):
        copy(r).wait()
        return c

    lax.fori_loop(0, GATHER_ROWS, start, 0)
    lax.fori_loop(0, GATHER_ROWS, wait, 0)


def _gather_rows(src, idx):
    n = idx.shape[0]
    return pl.pallas_call(
        _gather_kernel,
        grid=(n // GATHER_ROWS,),
        in_specs=[pl.BlockSpec((GATHER_ROWS,), lambda i: (i,), memory_space=pltpu.SMEM),
                  pl.BlockSpec(memory_space=pl.ANY)],
        out_specs=pl.BlockSpec(memory_space=pl.ANY),
        out_shape=jax.ShapeDtypeStruct((n, src.shape[1]), src.dtype),
        scratch_shapes=[pltpu.SemaphoreType.DMA(())],
        compiler_params=pltpu.CompilerParams(dimension_semantics=("arbitrary",),
                                             has_side_effects=True),
        name="row_gather",
    )(idx, src)


def _expert_kernel(te_ref, nu_ref, xs_ref, wg_ref, wu_ref, wd_ref, gs_ref, ys_ref, xb_ref, acc_ref):
    i, j = pl.program_id(0), pl.program_id(1)
    used = i < nu_ref[0]

    @pl.when(jnp.logical_and(used, j == 0))
    def _():
        xb_ref[...] = xs_ref[...].astype(BF16)
        acc_ref[...] = jnp.zeros_like(acc_ref)

    @pl.when(used)
    def _():
        xb = xb_ref[...]
        g = _dot(xb, wg_ref[0])
        u = _dot(xb, wu_ref[0])
        a = (g * _sigmoid(g) * u).astype(BF16)
        acc_ref[...] += _dot(a, wd_ref[0])

    last = j == pl.num_programs(1) - 1

    @pl.when(jnp.logical_and(used, last))
    def _():
        ys_ref[...] = acc_ref[...] * gs_ref[...]

    @pl.when(jnp.logical_and(jnp.logical_not(used), last))
    def _():
        ys_ref[...] = jnp.zeros_like(ys_ref)


def _experts(xs, gate_sorted, tile_expert, n_used, w_up, w_down):
    p = xs.shape[0]
    nck = FF_EXPERT // MOE_FF_CHUNK
    tm = MOE_ROW_TILE

    def chunk(i, j, te, nu):
        return jnp.where(i < nu[0], j, nck - 1)

    grid_spec = pltpu.PrefetchScalarGridSpec(
        num_scalar_prefetch=2,
        grid=(p // tm, nck),
        in_specs=[
            pl.BlockSpec((tm, D_MODEL), lambda i, j, te, nu: (i, 0)),
            pl.BlockSpec((1, D_MODEL, MOE_FF_CHUNK), lambda i, j, te, nu: (te[i], 0, chunk(i, j, te, nu))),
            pl.BlockSpec((1, D_MODEL, MOE_FF_CHUNK),
                         lambda i, j, te, nu: (te[i], 0, nck + chunk(i, j, te, nu))),
            pl.BlockSpec((1, MOE_FF_CHUNK, D_MODEL), lambda i, j, te, nu: (te[i], chunk(i, j, te, nu), 0)),
            pl.BlockSpec((tm, 1), lambda i, j, te, nu: (i, 0)),
        ],
        out_specs=pl.BlockSpec((tm, D_MODEL), lambda i, j, te, nu: (i, 0)),
        scratch_shapes=[pltpu.VMEM((tm, D_MODEL), BF16), pltpu.VMEM((tm, D_MODEL), F32)],
    )
    return pl.pallas_call(
        _expert_kernel,
        grid_spec=grid_spec,
        out_shape=jax.ShapeDtypeStruct((p, D_MODEL), F32),
        compiler_params=_cparams("arbitrary", "arbitrary"),
        name="moe_experts",
    )(tile_expert, n_used, xs, w_up, w_up, w_down, gate_sorted)


def _combine_kernel(x_ref, mod_ref, y1_ref, y2_ref, o_ref):
    o_ref[0] = x_ref[0] + mod_ref[0, 5:6, :] * (y1_ref[0, 0] + y2_ref[0, 0])


def _combine(x, mod_l, yg):
    b, t, _ = x.shape
    tm = min(ROW_TILE, t)
    row = pl.BlockSpec((1, tm, D_MODEL), lambda i, j: (i, j, 0))
    yg = yg.reshape(2, b, t, D_MODEL)
    return pl.pallas_call(
        _combine_kernel,
        grid=(b, t // tm),
        in_specs=[row, pl.BlockSpec((1, 6, D_MODEL), lambda i, j: (i, 0, 0)),
                  pl.BlockSpec((1, 1, tm, D_MODEL), lambda i, j: (0, i, j, 0)),
                  pl.BlockSpec((1, 1, tm, D_MODEL), lambda i, j: (1, i, j, 0))],
        out_specs=row,
        out_shape=jax.ShapeDtypeStruct(x.shape, F32),
        compiler_params=_cparams("parallel", "parallel"),
        name="moe_combine",
    )(x, mod_l, yg, yg)


def _moe(x, mod_l, ng, wr, w_up, w_down):
    b, t, _ = x.shape
    n = b * t
    h, gates, sel = _router(x, mod_l, ng, wr)
    gates = gates.reshape(n, LANES)[:, :N_EXPERTS]
    sel = sel.reshape(n, LANES)[:, :N_EXPERTS] > 0.5

    tm = MOE_ROW_TILE
    n_tiles = 2 * n // tm + N_EXPERTS
    p = n_tiles * tm
    sel_i = sel.astype(jnp.int32)
    rank = jnp.cumsum(sel_i, axis=0) - sel_i
    counts = jnp.sum(sel_i, axis=0)
    padded = (counts + tm - 1) // tm * tm
    ends = jnp.cumsum(padded)
    pos = (ends - padded)[None, :] + rank
    tok = jnp.broadcast_to(jnp.arange(n, dtype=jnp.int32)[:, None], pos.shape)
    src = jnp.zeros((p,), jnp.int32).at[jnp.where(sel, pos, p).reshape(-1)].set(
        tok.reshape(-1), mode="drop")
    gate_sorted = jnp.zeros((p,), F32).at[jnp.where(sel, pos, p).reshape(-1)].set(
        gates.reshape(-1), mode="drop")
    tile_expert = jnp.minimum(
        jnp.searchsorted(ends, jnp.arange(n_tiles, dtype=jnp.int32) * tm, side="right"),
        N_EXPERTS - 1).astype(jnp.int32)
    n_used = (ends[-1] // tm).astype(jnp.int32).reshape(1)
    pos1 = jnp.min(jnp.where(sel, pos, p), axis=1)
    pos2 = jnp.max(jnp.where(sel, pos, -1), axis=1)

    xs = _gather_rows(h.reshape(n, D_MODEL), src)
    ys = _experts(xs, gate_sorted.reshape(p, 1), tile_expert, n_used, w_up, w_down)
    yg = _gather_rows(ys, jnp.concatenate([pos1, pos2]).astype(jnp.int32))
    return _combine(x, mod_l, yg)


def _rope_tables(t, dim):
    inv = ROPE_THETA ** (-jnp.arange(0, dim, 2, dtype=F32) / dim)
    ang = jnp.arange(t, dtype=F32)[:, None] * inv[None, :]
    return jnp.cos(ang), jnp.sin(ang)


def _rope_lane_tables(cos, sin, gain, offs, scale):
    t, hw = cos.shape
    ones, zeros = jnp.ones((t, LANES), F32), jnp.zeros((t, LANES), F32)
    c, sa, sb = ones, zeros, zeros
    g_up, g_dn = jnp.roll(gain, hw), jnp.roll(gain, -hw)
    for o in offs:
        c = c.at[:, o:o + hw].set(cos).at[:, o + hw:o + 2 * hw].set(cos)
        sb = sb.at[:, o:o + hw].set(-sin)
        sa = sa.at[:, o + hw:o + 2 * hw].set(sin)
    return jnp.stack([c * gain, sa * g_up, sb * g_dn]) * scale


def _pad_lanes(v, width=LANES):
    return jnp.zeros((width,), F32).at[:v.shape[0]].set(v)


def _even_weights(w_in, qlg, q_up, kvlg, kv_up):
    kr = jnp.zeros((D_MODEL, LANES), F32).at[:, MLA_NOPE:MLA_QK].set(w_in[:, KR_OFF:KR_OFF + MLA_ROPE])
    w_in_p = jnp.concatenate([w_in[:, :KR_OFF], kr, w_in[:, KR_OFF + MLA_ROPE:]], axis=1)
    q_up_p = jnp.pad(q_up.reshape(MLA_Q_RANK, MLA_HEADS, MLA_QK),
                     ((0, 0), (0, 0), (0, LANES - MLA_QK))).reshape(MLA_Q_RANK, MLA_HEADS * LANES)
    kv3 = kv_up.reshape(MLA_KV_RANK, MLA_HEADS, MLA_NOPE + MLA_V)
    k_p = jnp.pad(kv3[:, :, :MLA_NOPE], ((0, 0), (0, 0), (0, LANES - MLA_NOPE)))
    kv_up_p = jnp.concatenate([k_p.reshape(MLA_KV_RANK, -1),
                               kv3[:, :, MLA_NOPE:].reshape(MLA_KV_RANK, -1)], axis=1)
    return (w_in_p.astype(BF16), qlg.reshape(1, -1), q_up_p.astype(BF16), kvlg.reshape(1, -1),
            kv_up_p.astype(BF16))


def _odd_weights(w_in):
    nq = WIN_Q_HEADS * HEAD_DIM
    nkv = WIN_KV_HEADS * HEAD_DIM
    dup = lambda w: jnp.concatenate([w.reshape(D_MODEL, WIN_KV_HEADS, 1, HEAD_DIM)] * 2,
                                    axis=2).reshape(D_MODEL, 2 * nkv)
    return jnp.concatenate([w_in[:, :nq], dup(w_in[:, nq:nq + nkv]), dup(w_in[:, nq + nkv:])],
                           axis=1).astype(BF16)


def _trunk(x, c, p):
    b, t, _ = x.shape
    mod = _ada_mod(c, p["ada_w"], p["ada_b"])
    cos_r, sin_r = _rope_tables(t, MLA_ROPE)
    cos_h, sin_h = _rope_tables(t, HEAD_DIM)
    pair_offs = (0, HEAD_DIM)
    for l in range(DEPTH):
        i = l // 2
        mod_l = mod[l]
        ng = p["norm_g"][l]
        if l % 2 == 0:
            g_q, g_k = p["even_mla_qk_g"][i, 0], p["even_mla_qk_g"][i, 1]
            g_qb, g_kb = p["even_dil_qk_g"][i, 0], p["even_dil_qk_g"][i, 1]
            tabs = (
                _rope_lane_tables(cos_r, sin_r, _pad_lanes(g_q), (MLA_NOPE,), MLA_QK ** -0.5),
                _rope_lane_tables(cos_r, sin_r, _pad_lanes(g_k).at[:MLA_NOPE].set(0.0), (MLA_NOPE,), 1.0),
                _rope_lane_tables(cos_h, sin_h, jnp.tile(g_qb, 2), pair_offs, HEAD_DIM ** -0.5),
                _rope_lane_tables(cos_h, sin_h, jnp.tile(g_kb, 2), pair_offs, 1.0),
            )
            gkn = _pad_lanes(g_k[:MLA_NOPE]).reshape(1, LANES)
            wts = _even_weights(p["even_w_in"][i], p["even_q_latent_g"][i], p["even_q_up"][i],
                                p["even_kv_latent_g"][i], p["even_kv_up"][i]) + (gkn,)
            qa, ka, va, qb, kb, vb = _even_in(x, mod_l, ng[0:1], wts, tabs)
            oa = _mla_attention(qa, ka, va)
            outs, lses = _dilated_attention(qb, kb, vb)
            x = _even_out(x, mod_l, oa, outs, lses, p["even_w_out"][i].astype(BF16))
            x = _dense_ffn(x, mod_l, ng[1:2], p["dense_w_up"][i].astype(BF16),
                           p["dense_w_down"][i].astype(BF16))
        else:
            g_q, g_k = p["odd_qk_g"][i, 0], p["odd_qk_g"][i, 1]
            tabs = (
                _rope_lane_tables(cos_h, sin_h, jnp.tile(g_q, 2), pair_offs, HEAD_DIM ** -0.5),
                _rope_lane_tables(cos_h, sin_h, jnp.tile(g_k, 2), pair_offs, 1.0),
            )
            q, k, v = _odd_in(x, mod_l, ng[0:1], _odd_weights(p["odd_w_in"][i]), tabs)
            sink = jnp.repeat(p["odd_sink"][i], HEAD_DIM).reshape(1, -1)
            (o,) = _band_attention(q, k, v, WIN_HALF, 2, 1, sink=sink)
            x = _odd_out(x, mod_l, o, p["odd_w_out"][i].astype(BF16))
            wr = jnp.zeros((D_MODEL, LANES), F32).at[:, :N_EXPERTS].set(p["moe_router"][i])
            x = _moe(x, mod_l, ng[1:2], wr, p["moe_w_up"][i].astype(BF16),
                     p["moe_w_down"][i].astype(BF16))
    return x


def kernel(x_prompt, x_sample, c_prompt, c_sample, ada_w, ada_b, norm_g, even_w_in, even_q_latent_g, even_q_up, even_kv_latent_g, even_kv_up, even_mla_qk_g, even_dil_qk_g, even_w_out, odd_w_in, odd_qk_g, odd_sink, odd_w_out, dense_w_up, dense_w_down, moe_router, moe_w_up, moe_w_down):
    p = dict(ada_w=ada_w, ada_b=ada_b, norm_g=norm_g, even_w_in=even_w_in,
             even_q_latent_g=even_q_latent_g, even_q_up=even_q_up,
             even_kv_latent_g=even_kv_latent_g, even_kv_up=even_kv_up,
             even_mla_qk_g=even_mla_qk_g, even_dil_qk_g=even_dil_qk_g, even_w_out=even_w_out,
             odd_w_in=odd_w_in, odd_qk_g=odd_qk_g, odd_sink=odd_sink, odd_w_out=odd_w_out,
             dense_w_up=dense_w_up, dense_w_down=dense_w_down, moe_router=moe_router,
             moe_w_up=moe_w_up, moe_w_down=moe_w_down)
    return _trunk(x_prompt, c_prompt, p), _trunk(x_sample, c_sample, p)
```

```python
import functools
import math

import jax
import jax.numpy as jnp
from jax import lax
from jax.experimental import pallas as pl
from jax.experimental.pallas import tpu as pltpu
from jax.experimental.pallas import tpu_sc as plsc

D_MODEL = 1024
DEPTH = 4
RMS_EPS = 1e-6
ROPE_THETA = 10000.0
NEG_INF = -1e30
HEAD_DIM = 64
MLA_HEADS = 8
MLA_NOPE = 64
MLA_ROPE = 32
MLA_QK = MLA_NOPE + MLA_ROPE
MLA_V = 64
MLA_Q_RANK = 384
MLA_KV_RANK = 256
DIL_HEADS = 8
DIL_PATTERNS = ((128, 1), (512, 4), (2048, 16))
WIN_Q_HEADS = 16
WIN_KV_HEADS = 4
WIN_HALF = 128
FF_DENSE = 2816
N_EXPERTS = 8
FF_EXPERT = 3584

LANES = 128
EVEN_PROJ = MLA_Q_RANK + MLA_KV_RANK + LANES + 3 * DIL_HEADS * HEAD_DIM
KR_OFF = MLA_Q_RANK + MLA_KV_RANK
DIL_OFF = KR_OFF + LANES
ODD_PROJ = WIN_Q_HEADS * HEAD_DIM + 2 * WIN_KV_HEADS * LANES

ROW_TILE = 512
MLA_Q_TILE = 256
BAND_Q_TILE = 128
MOE_ROW_TILE = 1024
MOE_FF_CHUNK = 512
DENSE_FF_CHUNK = 1408
VMEM_LIMIT = 56 * 1024 * 1024

BF16 = jnp.bfloat16
F32 = jnp.float32


def _cparams(*sem):
    return pltpu.CompilerParams(dimension_semantics=sem, vmem_limit_bytes=VMEM_LIMIT)


def _const_spec(shape):
    nd = len(shape)
    return pl.BlockSpec(shape, lambda *_: (0,) * nd, pipeline_mode=pl.Buffered(1))


def _dot(a, b):
    return jnp.dot(a, b, preferred_element_type=F32)


def _dot_t(a, b):
    return lax.dot_general(a, b, (((1,), (1,)), ((), ())), preferred_element_type=F32)


def _split_bf16(a):
    hi = a.astype(BF16)
    lo = (a - hi.astype(F32)).astype(BF16)
    return hi, lo


def _dot3(a, b):
    ah, al = _split_bf16(a)
    bh, bl = _split_bf16(b)
    return _dot(ah, bh) + (_dot(ah, bl) + _dot(al, bh))


def _sigmoid(x):
    return 1.0 / (1.0 + jnp.exp(-x))


HALF_D = D_MODEL // 2
HI16 = 0xFFFF0000


def _pack_row(x):
    bits = lambda v: lax.bitcast_convert_type(v.astype(BF16).astype(F32), jnp.uint32)
    return (bits(x[:, :HALF_D]) >> 16) | (bits(x[:, HALF_D:]) & jnp.uint32(HI16))


def _unpack_row(w):
    return (lax.bitcast_convert_type(w << 16, F32),
            lax.bitcast_convert_type(w & jnp.uint32(HI16), F32))


def _norm_mod(x, g, sc, sh):
    ms = jnp.mean(x * x, axis=-1, keepdims=True)
    return (x * lax.rsqrt(ms + RMS_EPS) * g) * (1.0 + sc) + sh


def _rms(x, g):
    ms = jnp.mean(x * x, axis=-1, keepdims=True)
    return x * lax.rsqrt(ms + RMS_EPS) * g


def _lane_lo(shape=(1, LANES)):
    return lax.broadcasted_iota(jnp.int32, shape, len(shape) - 1) < HEAD_DIM


def _rope_slab(slab, tab, shift):
    return (slab * tab[0] + pltpu.roll(slab, shift, 1) * tab[1]
            + pltpu.roll(slab, LANES - shift, 1) * tab[2])


def _norm_rope_pair(slab, tab):
    lo = _lane_lo()
    sq = slab * slab
    ss_all = jnp.sum(sq, axis=-1, keepdims=True)
    ss_lo = jnp.sum(jnp.where(lo, sq, 0.0), axis=-1, keepdims=True)
    r_lo = lax.rsqrt(ss_lo * (1.0 / HEAD_DIM) + RMS_EPS)
    r_hi = lax.rsqrt((ss_all - ss_lo) * (1.0 / HEAD_DIM) + RMS_EPS)
    return _rope_slab(slab, tab, HEAD_DIM // 2) * jnp.where(lo, r_lo, r_hi)


def _ada_kernel(c_ref, w_ref, b_ref, o_ref):
    c = c_ref[...]
    o_ref[0] = _dot3(c * _sigmoid(c), w_ref[0]) + b_ref[0]


def _ada_mod(c, ada_w, ada_b):
    bc = c.shape[0]
    nchunk = ada_w.shape[2] // D_MODEL
    out = pl.pallas_call(
        _ada_kernel,
        grid=(DEPTH, nchunk),
        in_specs=[
            pl.BlockSpec((bc, D_MODEL), lambda l, j: (0, 0)),
            pl.BlockSpec((1, D_MODEL, D_MODEL), lambda l, j: (l, 0, j)),
            pl.BlockSpec((1, 1, D_MODEL), lambda l, j: (l, 0, j)),
        ],
        out_specs=pl.BlockSpec((1, bc, D_MODEL), lambda l, j: (l, 0, j)),
        out_shape=jax.ShapeDtypeStruct((DEPTH, bc, nchunk * D_MODEL), F32),
        compiler_params=_cparams("parallel", "parallel"),
        name="ada_mod",
    )(c, ada_w, ada_b.reshape(DEPTH, 1, -1))
    return out.reshape(DEPTH, bc, nchunk, D_MODEL)


def _even_in_kernel(x_ref, mod_ref, ng_ref, w_in_ref, qlg_ref, q_up_ref, kvlg_ref, kv_up_ref,
                    gkn_ref, tq_ref, tk_ref, tqb_ref, tkb_ref,
                    qa_ref, ka_ref, va_ref, qb_ref, kb_ref, vb_ref):
    h = _norm_mod(x_ref[0], ng_ref[...], mod_ref[0, 1:2, :], mod_ref[0, 0:1, :]).astype(BF16)
    proj = _dot(h, w_in_ref[...])

    cq = _rms(proj[:, :MLA_Q_RANK], qlg_ref[...]).astype(BF16)
    qraw = _dot(cq, q_up_ref[...])
    tq = (tq_ref[0], tq_ref[1], tq_ref[2])
    for hd in range(MLA_HEADS):
        slab = qraw[:, hd * LANES:(hd + 1) * LANES]
        ss = jnp.sum(slab * slab, axis=-1, keepdims=True)
        rinv = lax.rsqrt(ss * (1.0 / MLA_QK) + RMS_EPS)
        qa_ref[0, :, hd * LANES:(hd + 1) * LANES] = (
            _rope_slab(slab, tq, MLA_ROPE // 2) * rinv).astype(BF16)

    ckv = _rms(proj[:, MLA_Q_RANK:KR_OFF], kvlg_ref[...]).astype(BF16)
    kvraw = _dot(ckv, kv_up_ref[...])
    kr = proj[:, KR_OFF:DIL_OFF]
    ss_kr = jnp.sum(kr * kr, axis=-1, keepdims=True)
    kr_rot = _rope_slab(kr, (tk_ref[0], tk_ref[1], tk_ref[2]), MLA_ROPE // 2)
    gkn = gkn_ref[...]
    for hd in range(MLA_HEADS):
        slab = kvraw[:, hd * LANES:(hd + 1) * LANES]
        ss = jnp.sum(slab * slab, axis=-1, keepdims=True) + ss_kr
        rinv = lax.rsqrt(ss * (1.0 / MLA_QK) + RMS_EPS)
        ka_ref[0, :, hd * LANES:(hd + 1) * LANES] = ((slab * gkn + kr_rot) * rinv).astype(BF16)
    va_ref[0] = kvraw[:, MLA_HEADS * LANES:].astype(BF16)

    tqb = (tqb_ref[0], tqb_ref[1], tqb_ref[2])
    tkb = (tkb_ref[0], tkb_ref[1], tkb_ref[2])
    nsl = DIL_HEADS * HEAD_DIM // LANES
    for j in range(nsl):
        qs = proj[:, DIL_OFF + j * LANES:DIL_OFF + (j + 1) * LANES]
        qb_ref[0, :, j * LANES:(j + 1) * LANES] = _norm_rope_pair(qs, tqb).astype(BF16)
        ks = proj[:, DIL_OFF + (nsl + j) * LANES:DIL_OFF + (nsl + j + 1) * LANES]
        kb_ref[0, :, j * LANES:(j + 1) * LANES] = _norm_rope_pair(ks, tkb).astype(BF16)
    vb_ref[0] = proj[:, DIL_OFF + 2 * nsl * LANES:].astype(BF16)


def _even_in(x, mod_l, ng, wts, tabs):
    b, t, _ = x.shape
    tm = min(ROW_TILE, t)
    w_in, qlg, q_up, kvlg, kv_up, gkn = wts
    row = lambda c: pl.BlockSpec((1, tm, c), lambda i, j: (i, j, 0))
    tab = pl.BlockSpec((3, tm, LANES), lambda i, j: (0, j, 0))
    hc = DIL_HEADS * HEAD_DIM
    outs = [jax.ShapeDtypeStruct((b, t, c), BF16)
            for c in (MLA_HEADS * LANES, MLA_HEADS * LANES, MLA_HEADS * MLA_V, hc, hc, hc)]
    return pl.pallas_call(
        _even_in_kernel,
        grid=(b, t // tm),
        in_specs=[
            row(D_MODEL),
            pl.BlockSpec((1, 6, D_MODEL), lambda i, j: (i, 0, 0)),
            _const_spec(ng.shape), _const_spec(w_in.shape), _const_spec(qlg.shape),
            _const_spec(q_up.shape), _const_spec(kvlg.shape), _const_spec(kv_up.shape),
            _const_spec(gkn.shape), tab, tab, tab, tab,
        ],
        out_specs=[row(s.shape[2]) for s in outs],
        out_shape=outs,
        compiler_params=_cparams("parallel", "parallel"),
        name="even_in",
    )(x, mod_l, ng, w_in, qlg, q_up, kvlg, kv_up, gkn, *tabs)


def _mla_kernel(q_ref, k_ref, v_ref, o_ref):
    v = v_ref[0]
    outs = []
    for hd in range(2):
        q = q_ref[0, :, hd * LANES:(hd + 1) * LANES]
        k = k_ref[0, :, hd * LANES:(hd + 1) * LANES]
        s = _dot_t(q, k)
        m = jnp.max(s, axis=-1, keepdims=True)
        p = jnp.exp2(s - m)
        l = jnp.sum(p, axis=-1, keepdims=True)
        outs.append(_dot(p.astype(BF16), v) * (1.0 / l))
    o_ref[0] = jnp.where(_lane_lo(), outs[0], outs[1]).astype(BF16)


def _mla_attention(q, k, v):
    b, t, _ = q.shape
    tq = min(MLA_Q_TILE, t)
    return pl.pallas_call(
        _mla_kernel,
        grid=(b, MLA_HEADS // 2, t // tq),
        in_specs=[
            pl.BlockSpec((1, tq, 2 * LANES), lambda i, c, j: (i, j, c)),
            pl.BlockSpec((1, t, 2 * LANES), lambda i, c, j: (i, 0, c)),
            pl.BlockSpec((1, t, LANES), lambda i, c, j: (i, 0, c)),
        ],
        out_specs=pl.BlockSpec((1, tq, LANES), lambda i, c, j: (i, j, c)),
        out_shape=jax.ShapeDtypeStruct((b, t, MLA_HEADS * MLA_V), BF16),
        compiler_params=_cparams("parallel", "parallel", "parallel"),
        name="mla_attn",
    )(q, k, v)


def _band_chain(qs_list, kw, vw, mask, tq):
    lo = _lane_lo()
    m_lo = jnp.where(lo, 1.0, 0.0).astype(BF16)
    m_hi = jnp.where(lo, 0.0, 1.0).astype(BF16)
    parts = []
    for qs in qs_list:
        parts += [qs * m_lo, qs * m_hi]
    s = mask(_dot_t(jnp.concatenate(parts, axis=0), kw))
    m = jnp.max(s, axis=-1, keepdims=True)
    p = jnp.exp(s - m)
    l = jnp.sum(p, axis=-1, keepdims=True)
    o = _dot(p.astype(BF16), vw) * (1.0 / l)
    lse = m + jnp.log(l)
    res = []
    for qi in range(len(qs_list)):
        a, b = 2 * qi * tq, (2 * qi + 1) * tq
        res.append((jnp.where(lo, o[a:a + tq], o[b:b + tq]),
                    jnp.where(lo, lse[a:a + tq], lse[b:b + tq])))
    return res


def _band_rel(rows, tq, win):
    return (lax.rem(lax.broadcasted_iota(jnp.int32, (rows, win), 0), tq)
            - lax.broadcasted_iota(jnp.int32, (rows, win), 1))


def _win_kernel(q_ref, k_ref, v_ref, sink_ref, o_ref, *, seq, tq, win):
    half, nq = WIN_HALF, 2
    rel = _band_rel(2 * nq * tq, tq, win)
    bias_mid = jnp.where(jnp.abs(rel + half) <= half, 0.0, NEG_INF)

    def tile(q0, ws, interior):
        if interior:
            mask = lambda s: s + bias_mid
        else:
            valid = jnp.abs(rel + (q0 - ws)) <= half
            mask = lambda s: jnp.where(valid, s, NEG_INF)
        qs = [q_ref[0, pl.ds(q0, tq), qi * LANES:(qi + 1) * LANES] for qi in range(nq)]
        res = _band_chain(qs, k_ref[0, pl.ds(ws, win), :], v_ref[0, pl.ds(ws, win), :], mask, tq)
        for qi, (o_slab, lse_slab) in enumerate(res):
            keep = _sigmoid(lse_slab - sink_ref[:, qi * LANES:(qi + 1) * LANES])
            o_ref[0, pl.ds(q0, tq), qi * LANES:(qi + 1) * LANES] = (o_slab * keep).astype(BF16)

    n_t = seq // tq
    tile(0, 0, False)
    if n_t > 1:
        tile(seq - tq, seq - win, False)

    def mid(ti, carry):
        q0 = pl.multiple_of(ti * tq, tq)
        tile(q0, pl.multiple_of(q0 - half, HEAD_DIM), True)
        return carry

    if n_t > 2:
        lax.fori_loop(1, n_t - 1, mid, 0, unroll=2)


def _win_attention(q, k, v, sink):
    b, seq, _ = q.shape
    tq = min(BAND_Q_TILE, seq)
    win = min(tq + 2 * WIN_HALF, seq)
    assert tq >= WIN_HALF or win == seq
    kern = functools.partial(_win_kernel, seq=seq, tq=tq, win=win)
    return pl.pallas_call(
        kern,
        grid=(b, WIN_KV_HEADS),
        in_specs=[
            pl.BlockSpec((1, seq, 2 * LANES), lambda i, c: (i, 0, c)),
            pl.BlockSpec((1, seq, LANES), lambda i, c: (i, 0, c)),
            pl.BlockSpec((1, seq, LANES), lambda i, c: (i, 0, c)),
            pl.BlockSpec((1, 2 * LANES), lambda i, c: (0, c)),
        ],
        out_specs=pl.BlockSpec((1, seq, 2 * LANES), lambda i, c: (i, 0, c)),
        out_shape=jax.ShapeDtypeStruct(q.shape, BF16),
        compiler_params=_cparams("parallel", "parallel"),
        name="win_attn",
    )(q, k, v, sink)


def _dil_kernel(q_ref, k_ref, v_ref, o_ref, qf, kf, vf, qd, kd, vd, acc, lse, of, *, seq):
    qf[...] = q_ref[0].astype(F32)
    kf[...] = k_ref[0].astype(F32)
    vf[...] = v_ref[0].astype(F32)
    for pi in (1, 2):
        d = DIL_PATTERNS[pi][1]
        ln = seq // d
        for r in range(d):
            for src, dst in ((qf, qd), (kf, kd), (vf, vd)):
                dst[pi - 1, r * ln:(r + 1) * ln, :] = src[pl.ds(r, ln, stride=d), :].astype(BF16)

    for pi, (window, d) in enumerate(DIL_PATTERNS):
        ln = seq // d
        half = window // (2 * d)
        tq = min(BAND_Q_TILE, ln)
        win = min(tq + 2 * half, ln)
        n_t = ln // tq
        align = min(tq, HEAD_DIM)
        rel = _band_rel(2 * tq, tq, win)
        if pi == 0:
            load = lambda ref, row, n: ref[0, pl.ds(row, n), :]
            srcs = (q_ref, k_ref, v_ref)
        else:
            load = lambda ref, row, n, pi=pi: ref[pi - 1, pl.ds(row, n), :]
            srcs = (qd, kd, vd)

        def body(i, carry, pi=pi, ln=ln, half=half, tq=tq, win=win, n_t=n_t, align=align,
                 rel=rel, load=load, srcs=srcs):
            r = i // n_t
            q0 = (i - r * n_t) * tq
            ws = 0 if win == ln else jnp.clip(q0 - half, 0, ln - win)
            valid = jnp.abs(rel + (q0 - ws)) <= half
            rq = pl.multiple_of(r * ln + q0, align)
            rk = pl.multiple_of(r * ln + ws, align)
            ((o_slab, lse_slab),) = _band_chain(
                [load(srcs[0], rq, tq)], load(srcs[1], rk, win), load(srcs[2], rk, win),
                lambda s: jnp.where(valid, s, NEG_INF), tq)
            acc[pi, pl.ds(rq, tq), :] = o_slab
            lse[pi, pl.ds(rq, tq), :] = lse_slab
            return carry

        lax.fori_loop(0, seq // tq, body, 0, unroll=min(4, seq // tq))

    d2, d3 = DIL_PATTERNS[1][1], DIL_PATTERNS[2][1]
    l2n, l3n = seq // d2, seq // d3
    for r in range(d3):
        s1 = pl.ds(r, l3n, stride=d3)
        s2 = pl.ds((r % d2) * l2n + r // d2, l3n, stride=d3 // d2)
        s3 = pl.ds(r * l3n, l3n)
        l1, l2, l3 = lse[0, s1, :], lse[1, s2, :], lse[2, s3, :]
        m = jnp.maximum(jnp.maximum(l1, l2), l3)
        e1, e2, e3 = jnp.exp(l1 - m), jnp.exp(l2 - m), jnp.exp(l3 - m)
        of[s1, :] = ((e1 * acc[0, s1, :] + e2 * acc[1, s2, :] + e3 * acc[2, s3, :])
                     * (1.0 / (e1 + e2 + e3)))
    o_ref[0] = of[...].astype(BF16)


def _dil_attention(q, k, v):
    b, seq, c = q.shape
    blk = pl.BlockSpec((1, seq, LANES), lambda i, j: (i, 0, j))
    f32s = pltpu.VMEM((seq, LANES), F32)
    res = pltpu.VMEM((2, seq, LANES), BF16)
    pat = pltpu.VMEM((len(DIL_PATTERNS), seq, LANES), F32)
    return pl.pallas_call(
        functools.partial(_dil_kernel, seq=seq),
        grid=(b, c // LANES),
        in_specs=[blk, blk, blk],
        out_specs=blk,
        out_shape=jax.ShapeDtypeStruct(q.shape, BF16),
        scratch_shapes=[f32s, f32s, f32s, res, res, res, pat, pat, f32s],
        compiler_params=_cparams("parallel", "parallel"),
        name="dil_attn",
    )(q, k, v)


def _even_out_kernel(x_ref, mod_ref, oa_ref, ob_ref, w_ref, y_ref):
    half = w_ref.shape[0] // 2
    mix = _dot(oa_ref[0], w_ref[:half, :]) + _dot(ob_ref[0], w_ref[half:, :])
    y_ref[0] = x_ref[0] + mod_ref[0, 2:3, :] * mix


def _even_out(x, mod_l, oa, ob, w_out):
    b, t, _ = x.shape
    tm = min(ROW_TILE, t)
    row = lambda c: pl.BlockSpec((1, tm, c), lambda i, j: (i, j, 0))
    return pl.pallas_call(
        _even_out_kernel,
        grid=(b, t // tm),
        in_specs=[row(D_MODEL), pl.BlockSpec((1, 6, D_MODEL), lambda i, j: (i, 0, 0)),
                  row(oa.shape[2]), row(ob.shape[2]), _const_spec(w_out.shape)],
        out_specs=row(D_MODEL),
        out_shape=jax.ShapeDtypeStruct(x.shape, F32),
        compiler_params=_cparams("parallel", "parallel"),
        name="even_out",
    )(x, mod_l, oa, ob, w_out)


def _dense_ffn_kernel(x_ref, mod_ref, ng_ref, wu_ref, wd_ref, y_ref):
    x = x_ref[0]
    h = _norm_mod(x, ng_ref[...], mod_ref[0, 4:5, :], mod_ref[0, 3:4, :]).astype(BF16)
    ff = wd_ref.shape[0]
    acc = None
    for c0 in range(0, ff, DENSE_FF_CHUNK):
        g = _dot(h, wu_ref[:, c0:c0 + DENSE_FF_CHUNK])
        u = _dot(h, wu_ref[:, ff + c0:ff + c0 + DENSE_FF_CHUNK])
        a = (g * _sigmoid(g) * u).astype(BF16)
        d = _dot(a, wd_ref[c0:c0 + DENSE_FF_CHUNK, :])
        acc = d if acc is None else acc + d
    y_ref[0] = x + mod_ref[0, 5:6, :] * acc


def _dense_ffn(x, mod_l, ng, w_up, w_down):
    b, t, _ = x.shape
    tm = min(ROW_TILE, t)
    row = pl.BlockSpec((1, tm, D_MODEL), lambda i, j: (i, j, 0))
    return pl.pallas_call(
        _dense_ffn_kernel,
        grid=(b, t // tm),
        in_specs=[row, pl.BlockSpec((1, 6, D_MODEL), lambda i, j: (i, 0, 0)),
                  _const_spec(ng.shape), _const_spec(w_up.shape), _const_spec(w_down.shape)],
        out_specs=row,
        out_shape=jax.ShapeDtypeStruct(x.shape, F32),
        compiler_params=_cparams("parallel", "parallel"),
        name="dense_ffn",
    )(x, mod_l, ng, w_up, w_down)


def _odd_in_kernel(x_ref, mod_ref, ng_ref, w_in_ref, tq_ref, tk_ref, q_ref, k_ref, v_ref):
    h = _norm_mod(x_ref[0], ng_ref[...], mod_ref[0, 1:2, :], mod_ref[0, 0:1, :]).astype(BF16)
    proj = _dot(h, w_in_ref[...])
    tq = (tq_ref[0], tq_ref[1], tq_ref[2])
    tk = (tk_ref[0], tk_ref[1], tk_ref[2])
    nq = WIN_Q_HEADS * HEAD_DIM // LANES
    for j in range(nq):
        q_ref[0, :, j * LANES:(j + 1) * LANES] = _norm_rope_pair(
            proj[:, j * LANES:(j + 1) * LANES], tq).astype(BF16)
    for j in range(WIN_KV_HEADS):
        c0 = (nq + j) * LANES
        k_ref[0, :, j * LANES:(j + 1) * LANES] = _norm_rope_pair(
            proj[:, c0:c0 + LANES], tk).astype(BF16)
    v_ref[0] = proj[:, (nq + WIN_KV_HEADS) * LANES:].astype(BF16)


def _odd_in(x, mod_l, ng, w_in, tabs):
    b, t, _ = x.shape
    tm = min(ROW_TILE, t)
    row = lambda c: pl.BlockSpec((1, tm, c), lambda i, j: (i, j, 0))
    tab = pl.BlockSpec((3, tm, LANES), lambda i, j: (0, j, 0))
    widths = (WIN_Q_HEADS * HEAD_DIM, WIN_KV_HEADS * LANES, WIN_KV_HEADS * LANES)
    return pl.pallas_call(
        _odd_in_kernel,
        grid=(b, t // tm),
        in_specs=[row(D_MODEL), pl.BlockSpec((1, 6, D_MODEL), lambda i, j: (i, 0, 0)),
                  _const_spec(ng.shape), _const_spec(w_in.shape), tab, tab],
        out_specs=[row(c) for c in widths],
        out_shape=[jax.ShapeDtypeStruct((b, t, c), BF16) for c in widths],
        compiler_params=_cparams("parallel", "parallel"),
        name="odd_in",
    )(x, mod_l, ng, w_in, *tabs)


def _odd_out_kernel(x_ref, mod_ref, o_ref, w_ref, y_ref):
    y_ref[0] = x_ref[0] + mod_ref[0, 2:3, :] * _dot(o_ref[0], w_ref[...])


def _odd_out(x, mod_l, o, w_out):
    b, t, _ = x.shape
    tm = min(ROW_TILE, t)
    row = pl.BlockSpec((1, tm, D_MODEL), lambda i, j: (i, j, 0))
    return pl.pallas_call(
        _odd_out_kernel,
        grid=(b, t // tm),
        in_specs=[row, pl.BlockSpec((1, 6, D_MODEL), lambda i, j: (i, 0, 0)), row,
                  _const_spec(w_out.shape)],
        out_specs=row,
        out_shape=jax.ShapeDtypeStruct(x.shape, F32),
        compiler_params=_cparams("parallel", "parallel"),
        name="odd_out",
    )(x, mod_l, o, w_out)


def _router_kernel(x_ref, mod_ref, ng_ref, wr_ref, h_ref, gate_ref, sel_ref):
    h = _norm_mod(x_ref[0], ng_ref[...], mod_ref[0, 4:5, :], mod_ref[0, 3:4, :])
    h_ref[0] = _pack_row(h)
    lane = lax.broadcasted_iota(jnp.int32, (1, LANES), 1).astype(F32)
    logits = jnp.where(lane < N_EXPERTS, _dot3(h, wr_ref[...]), NEG_INF)
    e = jnp.exp(logits - jnp.max(logits, axis=-1, keepdims=True))
    p = e * (1.0 / jnp.sum(e, axis=-1, keepdims=True))
    p = jnp.where(lane < N_EXPERTS, p, -1.0)
    m1 = jnp.max(p, axis=-1, keepdims=True)
    i1 = jnp.min(jnp.where(p == m1, lane, float(LANES)), axis=-1, keepdims=True)
    s1 = lane == i1
    p2 = jnp.where(s1, -1.0, p)
    m2 = jnp.max(p2, axis=-1, keepdims=True)
    i2 = jnp.min(jnp.where(p2 == m2, lane, float(LANES)), axis=-1, keepdims=True)
    s2 = lane == i2
    inv = 1.0 / (m1 + m2)
    gate_ref[0] = jnp.where(s1, m1 * inv, 0.0) + jnp.where(s2, m2 * inv, 0.0)
    sel_ref[0] = jnp.where(s1, 1.0, 0.0) + jnp.where(s2, 1.0, 0.0)


def _router(x, mod_l, ng, wr):
    b, t, _ = x.shape
    tm = min(ROW_TILE, t)
    row = lambda c: pl.BlockSpec((1, tm, c), lambda i, j: (i, j, 0))
    return pl.pallas_call(
        _router_kernel,
        grid=(b, t // tm),
        in_specs=[row(D_MODEL), pl.BlockSpec((1, 6, D_MODEL), lambda i, j: (i, 0, 0)),
                  _const_spec(ng.shape), _const_spec(wr.shape)],
        out_specs=[row(HALF_D), row(LANES), row(LANES)],
        out_shape=[jax.ShapeDtypeStruct((b, t, HALF_D), jnp.uint32),
                   jax.ShapeDtypeStruct((b, t, LANES), F32),
                   jax.ShapeDtypeStruct((b, t, LANES), F32)],
        compiler_params=_cparams("parallel", "parallel"),
        name="moe_router",
    )(x, mod_l, ng, wr)


SC_WINDOW = 128
SC_COLS = 256


def _sc_mesh():
    return plsc.VectorSubcoreMesh(core_axis_name="core", subcore_axis_name="subcore")


def _sc_index(pos, k):
    if k > 1:
        pos = (pos[:, None] * k + jnp.arange(k, dtype=jnp.int32)[None, :]).reshape(-1)
    return pos.reshape(1, -1).astype(jnp.int32)


def _scatter_rows_sc(src, pos_a, pos_b, n_out):
    n, width = src.shape
    k = width // SC_COLS
    src = src.reshape(n * k, SC_COLS)

    @pl.kernel(out_type=jax.ShapeDtypeStruct((n_out * k, SC_COLS), src.dtype), mesh=_sc_mesh(),
               scratch_types=[])
    def scatter(src_hbm, ia_hbm, ib_hbm, dst_hbm):
        def body(rows_vmem, ia_vmem, ib_vmem):
            pltpu.sync_copy(rows_vmem, dst_hbm.at[ia_vmem.at[0]])
            pltpu.sync_copy(rows_vmem, dst_hbm.at[ib_vmem.at[0]])

        idx_spec = pl.BlockSpec((1, SC_WINDOW), lambda i: (0, i))
        pltpu.emit_pipeline(
            body,
            grid=(n * k // SC_WINDOW,),
            in_specs=[pl.BlockSpec((SC_WINDOW, SC_COLS), lambda i: (i, 0)), idx_spec, idx_spec],
            out_specs=[],
            core_axis_name=("core", "subcore"),
            dimension_semantics=(pltpu.PARALLEL,),
        )(src_hbm, ia_hbm, ib_hbm)

    return scatter(src, _sc_index(pos_a, k), _sc_index(pos_b, k)).reshape(n_out, width)


def _gather_rows_sc(src, pos):
    n, width = pos.shape[0], src.shape[1]
    k = width // SC_COLS
    src = src.reshape(-1, SC_COLS)

    @pl.kernel(out_type=jax.ShapeDtypeStruct((n * k, SC_COLS), src.dtype), mesh=_sc_mesh(),
               scratch_types=[])
    def gather(src_hbm, i_hbm, dst_hbm):
        def body(i_vmem, rows_vmem):
            pltpu.sync_copy(src_hbm.at[i_vmem.at[0]], rows_vmem)

        pltpu.emit_pipeline(
            body,
            grid=(n * k // SC_WINDOW,),
            in_specs=[pl.BlockSpec((1, SC_WINDOW), lambda i: (0, i))],
            out_specs=[pl.BlockSpec((SC_WINDOW, SC_COLS), lambda i: (i, 0))],
            core_axis_name=("core", "subcore"),
            dimension_semantics=(pltpu.PARALLEL,),
        )(i_hbm, dst_hbm)

    return gather(src, _sc_index(pos, k)).reshape(n, width)


def _expert_kernel(te_ref, nu_ref, xs_ref, wg_ref, wu_ref, wd_ref, ys_ref, xb_ref, acc_ref):
    i, j = pl.program_id(0), pl.program_id(1)
    used = i < nu_ref[0]

    @pl.when(jnp.logical_and(used, j == 0))
    def _():
        lo, hi = _unpack_row(xs_ref[...])
        xb_ref[:, :HALF_D] = lo.astype(BF16)
        xb_ref[:, HALF_D:] = hi.astype(BF16)
        acc_ref[...] = jnp.zeros_like(acc_ref)

    @pl.when(used)
    def _():
        xb = xb_ref[...]
        g = _dot(xb, wg_ref[0])
        u = _dot(xb, wu_ref[0])
        a = (g * _sigmoid(g) * u).astype(BF16)
        acc_ref[...] += _dot(a, wd_ref[0])

    last = j == pl.num_programs(1) - 1

    @pl.when(jnp.logical_and(used, last))
    def _():
        ys_ref[...] = _pack_row(acc_ref[...])

    @pl.when(jnp.logical_and(jnp.logical_not(used), last))
    def _():
        ys_ref[...] = jnp.zeros_like(ys_ref)


def _experts(xs, tile_expert, n_used, w_up, w_down):
    p = xs.shape[0]
    nck = FF_EXPERT // MOE_FF_CHUNK
    tm = MOE_ROW_TILE

    def chunk(i, j, te, nu):
        return jnp.where(i < nu[0], j, nck - 1)

    grid_spec = pltpu.PrefetchScalarGridSpec(
        num_scalar_prefetch=2,
        grid=(p // tm, nck),
        in_specs=[
            pl.BlockSpec((tm, HALF_D), lambda i, j, te, nu: (i, 0)),
            pl.BlockSpec((1, D_MODEL, MOE_FF_CHUNK), lambda i, j, te, nu: (te[i], 0, chunk(i, j, te, nu))),
            pl.BlockSpec((1, D_MODEL, MOE_FF_CHUNK),
                         lambda i, j, te, nu: (te[i], 0, nck + chunk(i, j, te, nu))),
            pl.BlockSpec((1, MOE_FF_CHUNK, D_MODEL), lambda i, j, te, nu: (te[i], chunk(i, j, te, nu), 0)),
        ],
        out_specs=pl.BlockSpec((tm, HALF_D), lambda i, j, te, nu: (i, 0)),
        scratch_shapes=[pltpu.VMEM((tm, D_MODEL), BF16), pltpu.VMEM((tm, D_MODEL), F32)],
    )
    return pl.pallas_call(
        _expert_kernel,
        grid_spec=grid_spec,
        out_shape=jax.ShapeDtypeStruct((p, HALF_D), jnp.uint32),
        compiler_params=_cparams("arbitrary", "arbitrary"),
        name="moe_experts",
    )(tile_expert, n_used, xs, w_up, w_up, w_down)


def _combine_kernel(x_ref, mod_ref, g_ref, y1_ref, y2_ref, o_ref):
    g = g_ref[0]
    ga, gb = g[:, 0:1], g[:, 1:2]
    a_lo, a_hi = _unpack_row(y1_ref[0, 0])
    b_lo, b_hi = _unpack_row(y2_ref[0, 0])
    o_ref[0, :, :HALF_D] = x_ref[0, :, :HALF_D] + mod_ref[0, 5:6, :HALF_D] * (ga * a_lo + gb * b_lo)
    o_ref[0, :, HALF_D:] = x_ref[0, :, HALF_D:] + mod_ref[0, 5:6, HALF_D:] * (ga * a_hi + gb * b_hi)


def _combine(x, mod_l, gate_ab, yg):
    b, t, _ = x.shape
    tm = min(ROW_TILE, t)
    row = pl.BlockSpec((1, tm, D_MODEL), lambda i, j: (i, j, 0))
    yg = yg.reshape(2, b, t, HALF_D)
    return pl.pallas_call(
        _combine_kernel,
        grid=(b, t // tm),
        in_specs=[row, pl.BlockSpec((1, 6, D_MODEL), lambda i, j: (i, 0, 0)),
                  pl.BlockSpec((1, tm, 2), lambda i, j: (i, j, 0)),
                  pl.BlockSpec((1, 1, tm, HALF_D), lambda i, j: (0, i, j, 0)),
                  pl.BlockSpec((1, 1, tm, HALF_D), lambda i, j: (1, i, j, 0))],
        out_specs=row,
        out_shape=jax.ShapeDtypeStruct(x.shape, F32),
        compiler_params=_cparams("parallel", "parallel"),
        name="moe_combine",
    )(x, mod_l, gate_ab, yg, yg)


def _moe(x, mod_l, ng, wr, w_up, w_down):
    b, t, _ = x.shape
    n = b * t
    h, gates, sel = _router(x, mod_l, ng, wr)
    gates = gates.reshape(n, LANES)[:, :N_EXPERTS]
    sel = sel.reshape(n, LANES)[:, :N_EXPERTS] > 0.5

    tm = MOE_ROW_TILE
    n_tiles = 2 * n // tm + N_EXPERTS
    p = n_tiles * tm
    sel_i = sel.astype(jnp.int32)
    rank = jnp.cumsum(sel_i, axis=0) - sel_i
    counts = jnp.sum(sel_i, axis=0)
    padded = (counts + tm - 1) // tm * tm
    ends = jnp.cumsum(padded)
    pos = (ends - padded)[None, :] + rank
    tile_start = jnp.arange(n_tiles, dtype=jnp.int32) * tm
    tile_expert = jnp.minimum(jnp.sum((tile_start[:, None] >= ends[None, :]).astype(jnp.int32), axis=1),
                              N_EXPERTS - 1)
    n_used = (ends[-1] // tm).astype(jnp.int32).reshape(1)
    order = jnp.cumsum(sel_i, axis=1)
    first, second = sel & (order == 1), sel & (order == 2)
    pos_a = jnp.sum(jnp.where(first, pos, 0), axis=1)
    pos_b = jnp.sum(jnp.where(second, pos, 0), axis=1)
    gate_ab = jnp.stack([jnp.sum(jnp.where(first, gates, 0.0), axis=1),
                         jnp.sum(jnp.where(second, gates, 0.0), axis=1)], axis=1)

    xs = _scatter_rows_sc(h.reshape(n, HALF_D), pos_a, pos_b, p)
    ys = _experts(xs, tile_expert, n_used, w_up, w_down)
    yg = _gather_rows_sc(ys, jnp.concatenate([pos_a, pos_b]))
    return _combine(x, mod_l, gate_ab.reshape(b, t, 2), yg)


def _rope_tables(t, dim):
    inv = ROPE_THETA ** (-jnp.arange(0, dim, 2, dtype=F32) / dim)
    ang = jnp.arange(t, dtype=F32)[:, None] * inv[None, :]
    return jnp.cos(ang), jnp.sin(ang)


def _rope_lane_tables(cos, sin, gain, offs, scale):
    t, hw = cos.shape
    ones, zeros = jnp.ones((t, LANES), F32), jnp.zeros((t, LANES), F32)
    c, sa, sb = ones, zeros, zeros
    g_up, g_dn = jnp.roll(gain, hw), jnp.roll(gain, -hw)
    for o in offs:
        c = c.at[:, o:o + hw].set(cos).at[:, o + hw:o + 2 * hw].set(cos)
        sb = sb.at[:, o:o + hw].set(-sin)
        sa = sa.at[:, o + hw:o + 2 * hw].set(sin)
    return jnp.stack([c * gain, sa * g_up, sb * g_dn]) * scale


def _pad_lanes(v, width=LANES):
    return jnp.zeros((width,), F32).at[:v.shape[0]].set(v)


def _even_weights(w_in, qlg, q_up, kvlg, kv_up):
    kr = jnp.zeros((D_MODEL, LANES), F32).at[:, MLA_NOPE:MLA_QK].set(w_in[:, KR_OFF:KR_OFF + MLA_ROPE])
    w_in_p = jnp.concatenate([w_in[:, :KR_OFF], kr, w_in[:, KR_OFF + MLA_ROPE:]], axis=1)
    q_up_p = jnp.pad(q_up.reshape(MLA_Q_RANK, MLA_HEADS, MLA_QK),
                     ((0, 0), (0, 0), (0, LANES - MLA_QK))).reshape(MLA_Q_RANK, MLA_HEADS * LANES)
    kv3 = kv_up.reshape(MLA_KV_RANK, MLA_HEADS, MLA_NOPE + MLA_V)
    k_p = jnp.pad(kv3[:, :, :MLA_NOPE], ((0, 0), (0, 0), (0, LANES - MLA_NOPE)))
    kv_up_p = jnp.concatenate([k_p.reshape(MLA_KV_RANK, -1),
                               kv3[:, :, MLA_NOPE:].reshape(MLA_KV_RANK, -1)], axis=1)
    return (w_in_p.astype(BF16), qlg.reshape(1, -1), q_up_p.astype(BF16), kvlg.reshape(1, -1),
            kv_up_p.astype(BF16))


def _odd_weights(w_in):
    nq = WIN_Q_HEADS * HEAD_DIM
    nkv = WIN_KV_HEADS * HEAD_DIM
    dup = lambda w: jnp.concatenate([w.reshape(D_MODEL, WIN_KV_HEADS, 1, HEAD_DIM)] * 2,
                                    axis=2).reshape(D_MODEL, 2 * nkv)
    return jnp.concatenate([w_in[:, :nq], dup(w_in[:, nq:nq + nkv]), dup(w_in[:, nq + nkv:])],
                           axis=1).astype(BF16)


def _trunk(x, c, p):
    b, t, _ = x.shape
    mod = _ada_mod(c, p["ada_w"], p["ada_b"])
    cos_r, sin_r = _rope_tables(t, MLA_ROPE)
    cos_h, sin_h = _rope_tables(t, HEAD_DIM)
    pair_offs = (0, HEAD_DIM)
    for l in range(DEPTH):
        i = l // 2
        mod_l = mod[l]
        ng = p["norm_g"][l]
        if l % 2 == 0:
            g_q, g_k = p["even_mla_qk_g"][i, 0], p["even_mla_qk_g"][i, 1]
            g_qb, g_kb = p["even_dil_qk_g"][i, 0], p["even_dil_qk_g"][i, 1]
            tabs = (
                _rope_lane_tables(cos_r, sin_r, _pad_lanes(g_q), (MLA_NOPE,),
                                  MLA_QK ** -0.5 * math.log2(math.e)),
                _rope_lane_tables(cos_r, sin_r, _pad_lanes(g_k).at[:MLA_NOPE].set(0.0), (MLA_NOPE,), 1.0),
                _rope_lane_tables(cos_h, sin_h, jnp.tile(g_qb, 2), pair_offs, HEAD_DIM ** -0.5),
                _rope_lane_tables(cos_h, sin_h, jnp.tile(g_kb, 2), pair_offs, 1.0),
            )
            gkn = _pad_lanes(g_k[:MLA_NOPE]).reshape(1, LANES)
            wts = _even_weights(p["even_w_in"][i], p["even_q_latent_g"][i], p["even_q_up"][i],
                                p["even_kv_latent_g"][i], p["even_kv_up"][i]) + (gkn,)
            qa, ka, va, qb, kb, vb = _even_in(x, mod_l, ng[0:1], wts, tabs)
            oa = _mla_attention(qa, ka, va)
            ob = _dil_attention(qb, kb, vb)
            x = _even_out(x, mod_l, oa, ob, p["even_w_out"][i].astype(BF16))
            x = _dense_ffn(x, mod_l, ng[1:2], p["dense_w_up"][i].astype(BF16),
                           p["dense_w_down"][i].astype(BF16))
        else:
            g_q, g_k = p["odd_qk_g"][i, 0], p["odd_qk_g"][i, 1]
            tabs = (
                _rope_lane_tables(cos_h, sin_h, jnp.tile(g_q, 2), pair_offs, HEAD_DIM ** -0.5),
                _rope_lane_tables(cos_h, sin_h, jnp.tile(g_k, 2), pair_offs, 1.0),
            )
            q, k, v = _odd_in(x, mod_l, ng[0:1], _odd_weights(p["odd_w_in"][i]), tabs)
            sink = jnp.repeat(p["odd_sink"][i], HEAD_DIM).reshape(1, -1)
            o = _win_attention(q, k, v, sink)
            x = _odd_out(x, mod_l, o, p["odd_w_out"][i].astype(BF16))
            wr = jnp.zeros((D_MODEL, LANES), F32).at[:, :N_EXPERTS].set(p["moe_router"][i])
            x = _moe(x, mod_l, ng[1:2], wr, p["moe_w_up"][i].astype(BF16),
                     p["moe_w_down"][i].astype(BF16))
    return x


def kernel(x_prompt, x_sample, c_prompt, c_sample, ada_w, ada_b, norm_g, even_w_in, even_q_latent_g, even_q_up, even_kv_latent_g, even_kv_up, even_mla_qk_g, even_dil_qk_g, even_w_out, odd_w_in, odd_qk_g, odd_sink, odd_w_out, dense_w_up, dense_w_down, moe_router, moe_w_up, moe_w_down):
    p = dict(ada_w=ada_w, ada_b=ada_b, norm_g=norm_g, even_w_in=even_w_in,
             even_q_latent_g=even_q_latent_g, even_q_up=even_q_up,
             even_kv_latent_g=even_kv_latent_g, even_kv_up=even_kv_up,
             even_mla_qk_g=even_mla_qk_g, even_dil_qk_g=even_dil_qk_g, even_w_out=even_w_out,
             odd_w_in=odd_w_in, odd_qk_g=odd_qk_g, odd_sink=odd_sink, odd_w_out=odd_w_out,
             dense_w_up=dense_w_up, dense_w_down=dense_w_down, moe_router=moe_router,
             moe_w_up=moe_w_up, moe_w_down=moe_w_down)
    return _trunk(x_prompt, c_prompt, p), _trunk(x_sample, c_sample, p)
```

```python
import functools
import math

import jax
import jax.numpy as jnp
from jax import lax
from jax.experimental import pallas as pl
from jax.experimental.pallas import tpu as pltpu
from jax.experimental.pallas import tpu_sc as plsc

D_MODEL = 1024
DEPTH = 4
RMS_EPS = 1e-6
ROPE_THETA = 10000.0
NEG_INF = -1e30
HEAD_DIM = 64
MLA_HEADS = 8
MLA_NOPE = 64
MLA_ROPE = 32
MLA_QK = MLA_NOPE + MLA_ROPE
MLA_V = 64
MLA_Q_RANK = 384
MLA_KV_RANK = 256
DIL_HEADS = 8
DIL_PATTERNS = ((128, 1), (512, 4), (2048, 16))
WIN_Q_HEADS = 16
WIN_KV_HEADS = 4
WIN_HALF = 128
FF_DENSE = 2816
N_EXPERTS = 8
FF_EXPERT = 3584

LANES = 128
EVEN_PROJ = MLA_Q_RANK + MLA_KV_RANK + LANES + 3 * DIL_HEADS * HEAD_DIM
KR_OFF = MLA_Q_RANK + MLA_KV_RANK
DIL_OFF = KR_OFF + LANES
ODD_PROJ = WIN_Q_HEADS * HEAD_DIM + 2 * WIN_KV_HEADS * LANES

ROW_TILE = 512
MLA_Q_TILE = 256
BAND_Q_TILE = 128
MOE_ROW_TILE = 1024
MOE_FF_CHUNK = 512
DENSE_FF_CHUNK = 1408
VMEM_LIMIT = 56 * 1024 * 1024

BF16 = jnp.bfloat16
F32 = jnp.float32


def _cparams(*sem):
    return pltpu.CompilerParams(dimension_semantics=sem, vmem_limit_bytes=VMEM_LIMIT)


def _const_spec(shape):
    nd = len(shape)
    return pl.BlockSpec(shape, lambda *_: (0,) * nd, pipeline_mode=pl.Buffered(1))


def _dot(a, b):
    return jnp.dot(a, b, preferred_element_type=F32)


def _dot_t(a, b):
    return lax.dot_general(a, b, (((1,), (1,)), ((), ())), preferred_element_type=F32)


def _split_bf16(a):
    hi = a.astype(BF16)
    lo = (a - hi.astype(F32)).astype(BF16)
    return hi, lo


def _dot3(a, b):
    ah, al = _split_bf16(a)
    bh, bl = _split_bf16(b)
    return _dot(ah, bh) + (_dot(ah, bl) + _dot(al, bh))


def _sigmoid(x):
    return 1.0 / (1.0 + jnp.exp(-x))


HALF_D = D_MODEL // 2
HI16 = 0xFFFF0000
PLANE_W = HALF_D // 2


def _pack_row(x):
    bits = lambda v: lax.bitcast_convert_type(v.astype(BF16).astype(F32), jnp.uint32)
    return (bits(x[:, :HALF_D]) >> 16) | (bits(x[:, HALF_D:]) & jnp.uint32(HI16))


def _unpack_row(w):
    return (lax.bitcast_convert_type(w << 16, F32),
            lax.bitcast_convert_type(w & jnp.uint32(HI16), F32))


def _norm_mod(x, g, sc, sh):
    ms = jnp.mean(x * x, axis=-1, keepdims=True)
    return (x * lax.rsqrt(ms + RMS_EPS) * g) * (1.0 + sc) + sh


def _rms(x, g):
    ms = jnp.mean(x * x, axis=-1, keepdims=True)
    return x * lax.rsqrt(ms + RMS_EPS) * g


def _lane_lo(shape=(1, LANES)):
    return lax.broadcasted_iota(jnp.int32, shape, len(shape) - 1) < HEAD_DIM


def _rope_slab(slab, tab, shift):
    return (slab * tab[0] + pltpu.roll(slab, shift, 1) * tab[1]
            + pltpu.roll(slab, LANES - shift, 1) * tab[2])


def _norm_rope_pair(slab, tab):
    lo = _lane_lo()
    sq = slab * slab
    ss_all = jnp.sum(sq, axis=-1, keepdims=True)
    ss_lo = jnp.sum(jnp.where(lo, sq, 0.0), axis=-1, keepdims=True)
    r_lo = lax.rsqrt(ss_lo * (1.0 / HEAD_DIM) + RMS_EPS)
    r_hi = lax.rsqrt((ss_all - ss_lo) * (1.0 / HEAD_DIM) + RMS_EPS)
    return _rope_slab(slab, tab, HEAD_DIM // 2) * jnp.where(lo, r_lo, r_hi)


def _ada_kernel(c_ref, w_ref, b_ref, o_ref):
    c = c_ref[...]
    o_ref[0] = _dot3(c * _sigmoid(c), w_ref[0]) + b_ref[0]


def _ada_mod(c, ada_w, ada_b):
    bc = c.shape[0]
    nchunk = ada_w.shape[2] // D_MODEL
    out = pl.pallas_call(
        _ada_kernel,
        grid=(DEPTH, nchunk),
        in_specs=[
            pl.BlockSpec((bc, D_MODEL), lambda l, j: (0, 0)),
            pl.BlockSpec((1, D_MODEL, D_MODEL), lambda l, j: (l, 0, j)),
            pl.BlockSpec((1, 1, D_MODEL), lambda l, j: (l, 0, j)),
        ],
        out_specs=pl.BlockSpec((1, bc, D_MODEL), lambda l, j: (l, 0, j)),
        out_shape=jax.ShapeDtypeStruct((DEPTH, bc, nchunk * D_MODEL), F32),
        compiler_params=_cparams("parallel", "parallel"),
        name="ada_mod",
    )(c, ada_w, ada_b.reshape(DEPTH, 1, -1))
    return out.reshape(DEPTH, bc, nchunk, D_MODEL)


def _even_in_kernel(x_ref, mod_ref, ng_ref, w_in_ref, qlg_ref, q_up_ref, kvlg_ref, kv_up_ref,
                    gkn_ref, tq_ref, tk_ref, tqb_ref, tkb_ref,
                    qa_ref, ka_ref, va_ref, qb_ref, kb_ref, vb_ref):
    h = _norm_mod(x_ref[0], ng_ref[...], mod_ref[0, 1:2, :], mod_ref[0, 0:1, :]).astype(BF16)
    proj = _dot(h, w_in_ref[...])

    cq = _rms(proj[:, :MLA_Q_RANK], qlg_ref[...]).astype(BF16)
    qraw = _dot(cq, q_up_ref[...])
    tq = (tq_ref[0], tq_ref[1], tq_ref[2])
    for hd in range(MLA_HEADS):
        slab = qraw[:, hd * LANES:(hd + 1) * LANES]
        ss = jnp.sum(slab * slab, axis=-1, keepdims=True)
        rinv = lax.rsqrt(ss * (1.0 / MLA_QK) + RMS_EPS)
        qa_ref[0, :, hd * LANES:(hd + 1) * LANES] = (
            _rope_slab(slab, tq, MLA_ROPE // 2) * rinv).astype(BF16)

    ckv = _rms(proj[:, MLA_Q_RANK:KR_OFF], kvlg_ref[...]).astype(BF16)
    kvraw = _dot(ckv, kv_up_ref[...])
    kr = proj[:, KR_OFF:DIL_OFF]
    ss_kr = jnp.sum(kr * kr, axis=-1, keepdims=True)
    kr_rot = _rope_slab(kr, (tk_ref[0], tk_ref[1], tk_ref[2]), MLA_ROPE // 2)
    gkn = gkn_ref[...]
    for hd in range(MLA_HEADS):
        slab = kvraw[:, hd * LANES:(hd + 1) * LANES]
        ss = jnp.sum(slab * slab, axis=-1, keepdims=True) + ss_kr
        rinv = lax.rsqrt(ss * (1.0 / MLA_QK) + RMS_EPS)
        ka_ref[0, :, hd * LANES:(hd + 1) * LANES] = ((slab * gkn + kr_rot) * rinv).astype(BF16)
    va_ref[0] = kvraw[:, MLA_HEADS * LANES:].astype(BF16)

    tqb = (tqb_ref[0], tqb_ref[1], tqb_ref[2])
    tkb = (tkb_ref[0], tkb_ref[1], tkb_ref[2])
    nsl = DIL_HEADS * HEAD_DIM // LANES
    for j in range(nsl):
        qs = proj[:, DIL_OFF + j * LANES:DIL_OFF + (j + 1) * LANES]
        qb_ref[0, :, j * LANES:(j + 1) * LANES] = _norm_rope_pair(qs, tqb).astype(BF16)
        ks = proj[:, DIL_OFF + (nsl + j) * LANES:DIL_OFF + (nsl + j + 1) * LANES]
        kb_ref[0, :, j * LANES:(j + 1) * LANES] = _norm_rope_pair(ks, tkb).astype(BF16)
    vb_ref[0] = proj[:, DIL_OFF + 2 * nsl * LANES:].astype(BF16)


def _even_in(x, mod_l, ng, wts, tabs):
    b, t, _ = x.shape
    tm = min(ROW_TILE, t)
    w_in, qlg, q_up, kvlg, kv_up, gkn = wts
    row = lambda c: pl.BlockSpec((1, tm, c), lambda i, j: (i, j, 0))
    tab = pl.BlockSpec((3, tm, LANES), lambda i, j: (0, j, 0))
    hc = DIL_HEADS * HEAD_DIM
    outs = [jax.ShapeDtypeStruct((b, t, c), BF16)
            for c in (MLA_HEADS * LANES, MLA_HEADS * LANES, MLA_HEADS * MLA_V, hc, hc, hc)]
    return pl.pallas_call(
        _even_in_kernel,
        grid=(b, t // tm),
        in_specs=[
            row(D_MODEL),
            pl.BlockSpec((1, 6, D_MODEL), lambda i, j: (i, 0, 0)),
            _const_spec(ng.shape), _const_spec(w_in.shape), _const_spec(qlg.shape),
            _const_spec(q_up.shape), _const_spec(kvlg.shape), _const_spec(kv_up.shape),
            _const_spec(gkn.shape), tab, tab, tab, tab,
        ],
        out_specs=[row(s.shape[2]) for s in outs],
        out_shape=outs,
        compiler_params=_cparams("parallel", "parallel"),
        name="even_in",
    )(x, mod_l, ng, w_in, qlg, q_up, kvlg, kv_up, gkn, *tabs)


def _mla_kernel(q_ref, k_ref, v_ref, o_ref):
    lo = _lane_lo()
    v = v_ref[0]
    one = jnp.ones((), BF16)
    v_aug = (jnp.where(lo, v, one), jnp.where(lo, one, v))
    outs = []
    for hd in range(2):
        q = q_ref[0, :, hd * LANES:(hd + 1) * LANES]
        k = k_ref[0, :, hd * LANES:(hd + 1) * LANES]
        s = _dot_t(q, k)
        m = jnp.max(s, axis=-1, keepdims=True)
        p = jnp.exp2((s - m).astype(BF16))
        outs.append(_dot(p, v_aug[hd]))
    num = jnp.where(lo, outs[0], outs[1])
    den = jnp.where(lo, pltpu.roll(outs[0], HEAD_DIM, 1), pltpu.roll(outs[1], HEAD_DIM, 1))
    o_ref[0] = (num * (1.0 / den)).astype(BF16)


def _mla_attention(q, k, v):
    b, t, _ = q.shape
    tq = min(MLA_Q_TILE, t)
    return pl.pallas_call(
        _mla_kernel,
        grid=(b, MLA_HEADS // 2, t // tq),
        in_specs=[
            pl.BlockSpec((1, tq, 2 * LANES), lambda i, c, j: (i, j, c)),
            pl.BlockSpec((1, t, 2 * LANES), lambda i, c, j: (i, 0, c)),
            pl.BlockSpec((1, t, LANES), lambda i, c, j: (i, 0, c)),
        ],
        out_specs=pl.BlockSpec((1, tq, LANES), lambda i, c, j: (i, j, c)),
        out_shape=jax.ShapeDtypeStruct((b, t, MLA_HEADS * MLA_V), BF16),
        compiler_params=_cparams("parallel", "parallel", "parallel"),
        name="mla_attn",
    )(q, k, v)


def _band_chain(qs_list, kw, vw, mask, tq):
    lo = _lane_lo()
    m_lo = jnp.where(lo, 1.0, 0.0).astype(BF16)
    m_hi = jnp.where(lo, 0.0, 1.0).astype(BF16)
    parts = []
    for qs in qs_list:
        parts += [qs * m_lo, qs * m_hi]
    s = mask(_dot_t(jnp.concatenate(parts, axis=0), kw))
    m = jnp.max(s, axis=-1, keepdims=True)
    p = jnp.exp2(s - m)
    l = jnp.sum(p, axis=-1, keepdims=True)
    o = _dot(p.astype(BF16), vw) * (1.0 / l)
    lse = (m + jnp.log2(l)) * math.log(2.0)
    res = []
    for qi in range(len(qs_list)):
        a, b = 2 * qi * tq, (2 * qi + 1) * tq
        res.append((jnp.where(lo, o[a:a + tq], o[b:b + tq]),
                    jnp.where(lo, lse[a:a + tq], lse[b:b + tq])))
    return res


def _band_rel(rows, tq, win, half):
    return (lax.rem(lax.broadcasted_iota(jnp.int32, (rows, win), 0), tq)
            - lax.broadcasted_iota(jnp.int32, (rows, win), 1)) + half


def _band_valid(relh, off, half):
    return lax.bitcast_convert_type(relh + off, jnp.uint32) <= jnp.uint32(2 * half)


def _win_kernel(q_ref, k_ref, v_ref, sink_ref, o_ref, *, seq, tq, win):
    half, nq = WIN_HALF, 2
    relh = _band_rel(2 * nq * tq, tq, win, half)
    bias_mid = jnp.where(_band_valid(relh, half, half), 0.0, NEG_INF)

    def tile(q0, ws, interior):
        if interior:
            mask = lambda s: s + bias_mid
        else:
            valid = _band_valid(relh, q0 - ws, half)
            mask = lambda s: jnp.where(valid, s, NEG_INF)
        qs = [q_ref[0, pl.ds(q0, tq), qi * LANES:(qi + 1) * LANES] for qi in range(nq)]
        res = _band_chain(qs, k_ref[0, pl.ds(ws, win), :], v_ref[0, pl.ds(ws, win), :], mask, tq)
        for qi, (o_slab, lse_slab) in enumerate(res):
            keep = _sigmoid(lse_slab - sink_ref[:, qi * LANES:(qi + 1) * LANES])
            o_ref[0, pl.ds(q0, tq), qi * LANES:(qi + 1) * LANES] = (o_slab * keep).astype(BF16)

    n_t = seq // tq
    tile(0, 0, False)
    if n_t > 1:
        tile(seq - tq, seq - win, False)

    def mid(ti, carry):
        q0 = pl.multiple_of(ti * tq, tq)
        tile(q0, pl.multiple_of(q0 - half, HEAD_DIM), True)
        return carry

    if n_t > 2:
        lax.fori_loop(1, n_t - 1, mid, 0, unroll=2)


def _win_attention(q, k, v, sink):
    b, seq, _ = q.shape
    tq = min(BAND_Q_TILE, seq)
    win = min(tq + 2 * WIN_HALF, seq)
    assert tq >= WIN_HALF or win == seq
    kern = functools.partial(_win_kernel, seq=seq, tq=tq, win=win)
    return pl.pallas_call(
        kern,
        grid=(b, WIN_KV_HEADS),
        in_specs=[
            pl.BlockSpec((1, seq, 2 * LANES), lambda i, c: (i, 0, c)),
            pl.BlockSpec((1, seq, LANES), lambda i, c: (i, 0, c)),
            pl.BlockSpec((1, seq, LANES), lambda i, c: (i, 0, c)),
            pl.BlockSpec((1, 2 * LANES), lambda i, c: (0, c)),
        ],
        out_specs=pl.BlockSpec((1, seq, 2 * LANES), lambda i, c: (i, 0, c)),
        out_shape=jax.ShapeDtypeStruct(q.shape, BF16),
        compiler_params=_cparams("parallel", "parallel"),
        name="win_attn",
    )(q, k, v, sink)


def _dil_kernel(q_ref, k_ref, v_ref, o_ref, qf, kf, vf, qd, kd, vd, acc, lse, of, *, seq):
    qf[...] = q_ref[0].astype(F32)
    kf[...] = k_ref[0].astype(F32)
    vf[...] = v_ref[0].astype(F32)
    for pi in (1, 2):
        d = DIL_PATTERNS[pi][1]
        ln = seq // d
        for r in range(d):
            for src, dst in ((qf, qd), (kf, kd), (vf, vd)):
                dst[pi - 1, r * ln:(r + 1) * ln, :] = src[pl.ds(r, ln, stride=d), :].astype(BF16)

    for pi, (window, d) in enumerate(DIL_PATTERNS):
        ln = seq // d
        half = window // (2 * d)
        tq = min(BAND_Q_TILE, ln)
        win = min(tq + 2 * half, ln)
        n_t = ln // tq
        align = min(tq, HEAD_DIM)
        rel = _band_rel(2 * tq, tq, win, half)
        if pi == 0:
            load = lambda ref, row, n: ref[0, pl.ds(row, n), :]
            srcs = (q_ref, k_ref, v_ref)
        else:
            load = lambda ref, row, n, pi=pi: ref[pi - 1, pl.ds(row, n), :]
            srcs = (qd, kd, vd)

        def body(i, carry, pi=pi, ln=ln, half=half, tq=tq, win=win, n_t=n_t, align=align,
                 rel=rel, load=load, srcs=srcs):
            r = i // n_t
            q0 = (i - r * n_t) * tq
            ws = 0 if win == ln else jnp.clip(q0 - half, 0, ln - win)
            valid = _band_valid(rel, q0 - ws, half)
            rq = pl.multiple_of(r * ln + q0, align)
            rk = pl.multiple_of(r * ln + ws, align)
            ((o_slab, lse_slab),) = _band_chain(
                [load(srcs[0], rq, tq)], load(srcs[1], rk, win), load(srcs[2], rk, win),
                lambda s: jnp.where(valid, s, NEG_INF), tq)
            acc[pi, pl.ds(rq, tq), :] = o_slab
            lse[pi, pl.ds(rq, tq), :] = lse_slab
            return carry

        lax.fori_loop(0, seq // tq, body, 0, unroll=min(4, seq // tq))

    d2, d3 = DIL_PATTERNS[1][1], DIL_PATTERNS[2][1]
    l2n, l3n = seq // d2, seq // d3
    for r in range(d3):
        s1 = pl.ds(r, l3n, stride=d3)
        s2 = pl.ds((r % d2) * l2n + r // d2, l3n, stride=d3 // d2)
        s3 = pl.ds(r * l3n, l3n)
        l1, l2, l3 = lse[0, s1, :], lse[1, s2, :], lse[2, s3, :]
        m = jnp.maximum(jnp.maximum(l1, l2), l3)
        e1, e2, e3 = jnp.exp(l1 - m), jnp.exp(l2 - m), jnp.exp(l3 - m)
        of[s1, :] = ((e1 * acc[0, s1, :] + e2 * acc[1, s2, :] + e3 * acc[2, s3, :])
                     * (1.0 / (e1 + e2 + e3)))
    o_ref[0] = of[...].astype(BF16)


def _dil_attention(q, k, v):
    b, seq, c = q.shape
    blk = pl.BlockSpec((1, seq, LANES), lambda i, j: (i, 0, j))
    f32s = pltpu.VMEM((seq, LANES), F32)
    res = pltpu.VMEM((2, seq, LANES), BF16)
    pat = pltpu.VMEM((len(DIL_PATTERNS), seq, LANES), F32)
    return pl.pallas_call(
        functools.partial(_dil_kernel, seq=seq),
        grid=(b, c // LANES),
        in_specs=[blk, blk, blk],
        out_specs=blk,
        out_shape=jax.ShapeDtypeStruct(q.shape, BF16),
        scratch_shapes=[f32s, f32s, f32s, res, res, res, pat, pat, f32s],
        compiler_params=_cparams("parallel", "parallel"),
        name="dil_attn",
    )(q, k, v)


def _even_out_kernel(x_ref, mod_ref, oa_ref, ob_ref, w_ref, y_ref):
    half = w_ref.shape[0] // 2
    mix = _dot(oa_ref[0], w_ref[:half, :]) + _dot(ob_ref[0], w_ref[half:, :])
    y_ref[0] = x_ref[0] + mod_ref[0, 2:3, :] * mix


def _even_out(x, mod_l, oa, ob, w_out):
    b, t, _ = x.shape
    tm = min(ROW_TILE, t)
    row = lambda c: pl.BlockSpec((1, tm, c), lambda i, j: (i, j, 0))
    return pl.pallas_call(
        _even_out_kernel,
        grid=(b, t // tm),
        in_specs=[row(D_MODEL), pl.BlockSpec((1, 6, D_MODEL), lambda i, j: (i, 0, 0)),
                  row(oa.shape[2]), row(ob.shape[2]), _const_spec(w_out.shape)],
        out_specs=row(D_MODEL),
        out_shape=jax.ShapeDtypeStruct(x.shape, F32),
        compiler_params=_cparams("parallel", "parallel"),
        name="even_out",
    )(x, mod_l, oa, ob, w_out)


def _dense_ffn_kernel(x_ref, mod_ref, ng_ref, wu_ref, wd_ref, y_ref):
    x = x_ref[0]
    h = _norm_mod(x, ng_ref[...], mod_ref[0, 4:5, :], mod_ref[0, 3:4, :]).astype(BF16)
    ff = wd_ref.shape[0]
    acc = None
    for c0 in range(0, ff, DENSE_FF_CHUNK):
        g = _dot(h, wu_ref[:, c0:c0 + DENSE_FF_CHUNK])
        u = _dot(h, wu_ref[:, ff + c0:ff + c0 + DENSE_FF_CHUNK])
        a = (g * _sigmoid(g) * u).astype(BF16)
        d = _dot(a, wd_ref[c0:c0 + DENSE_FF_CHUNK, :])
        acc = d if acc is None else acc + d
    y_ref[0] = x + mod_ref[0, 5:6, :] * acc


def _dense_ffn(x, mod_l, ng, w_up, w_down):
    b, t, _ = x.shape
    tm = min(ROW_TILE, t)
    row = pl.BlockSpec((1, tm, D_MODEL), lambda i, j: (i, j, 0))
    return pl.pallas_call(
        _dense_ffn_kernel,
        grid=(b, t // tm),
        in_specs=[row, pl.BlockSpec((1, 6, D_MODEL), lambda i, j: (i, 0, 0)),
                  _const_spec(ng.shape), _const_spec(w_up.shape), _const_spec(w_down.shape)],
        out_specs=row,
        out_shape=jax.ShapeDtypeStruct(x.shape, F32),
        compiler_params=_cparams("parallel", "parallel"),
        name="dense_ffn",
    )(x, mod_l, ng, w_up, w_down)


def _odd_in_kernel(x_ref, mod_ref, ng_ref, w_in_ref, tq_ref, tk_ref, q_ref, k_ref, v_ref):
    h = _norm_mod(x_ref[0], ng_ref[...], mod_ref[0, 1:2, :], mod_ref[0, 0:1, :]).astype(BF16)
    proj = _dot(h, w_in_ref[...])
    tq = (tq_ref[0], tq_ref[1], tq_ref[2])
    tk = (tk_ref[0], tk_ref[1], tk_ref[2])
    nq = WIN_Q_HEADS * HEAD_DIM // LANES
    for j in range(nq):
        q_ref[0, :, j * LANES:(j + 1) * LANES] = _norm_rope_pair(
            proj[:, j * LANES:(j + 1) * LANES], tq).astype(BF16)
    for j in range(WIN_KV_HEADS):
        c0 = (nq + j) * LANES
        k_ref[0, :, j * LANES:(j + 1) * LANES] = _norm_rope_pair(
            proj[:, c0:c0 + LANES], tk).astype(BF16)
    v_ref[0] = proj[:, (nq + WIN_KV_HEADS) * LANES:].astype(BF16)


def _odd_in(x, mod_l, ng, w_in, tabs):
    b, t, _ = x.shape
    tm = min(ROW_TILE, t)
    row = lambda c: pl.BlockSpec((1, tm, c), lambda i, j: (i, j, 0))
    tab = pl.BlockSpec((3, tm, LANES), lambda i, j: (0, j, 0))
    widths = (WIN_Q_HEADS * HEAD_DIM, WIN_KV_HEADS * LANES, WIN_KV_HEADS * LANES)
    return pl.pallas_call(
        _odd_in_kernel,
        grid=(b, t // tm),
        in_specs=[row(D_MODEL), pl.BlockSpec((1, 6, D_MODEL), lambda i, j: (i, 0, 0)),
                  _const_spec(ng.shape), _const_spec(w_in.shape), tab, tab],
        out_specs=[row(c) for c in widths],
        out_shape=[jax.ShapeDtypeStruct((b, t, c), BF16) for c in widths],
        compiler_params=_cparams("parallel", "parallel"),
        name="odd_in",
    )(x, mod_l, ng, w_in, *tabs)


def _odd_out_kernel(x_ref, mod_ref, o_ref, w_ref, y_ref):
    y_ref[0] = x_ref[0] + mod_ref[0, 2:3, :] * _dot(o_ref[0], w_ref[...])


def _odd_out(x, mod_l, o, w_out):
    b, t, _ = x.shape
    tm = min(ROW_TILE, t)
    row = pl.BlockSpec((1, tm, D_MODEL), lambda i, j: (i, j, 0))
    return pl.pallas_call(
        _odd_out_kernel,
        grid=(b, t // tm),
        in_specs=[row, pl.BlockSpec((1, 6, D_MODEL), lambda i, j: (i, 0, 0)), row,
                  _const_spec(w_out.shape)],
        out_specs=row,
        out_shape=jax.ShapeDtypeStruct(x.shape, F32),
        compiler_params=_cparams("parallel", "parallel"),
        name="odd_out",
    )(x, mod_l, o, w_out)


def _router_kernel(x_ref, mod_ref, ng_ref, wr_ref, h_ref, gate_ref, sel_ref):
    h = _norm_mod(x_ref[0], ng_ref[...], mod_ref[0, 4:5, :], mod_ref[0, 3:4, :])
    w = _pack_row(h)
    h_ref[0, 0] = w[:, :PLANE_W]
    h_ref[1, 0] = w[:, PLANE_W:]
    lane = lax.broadcasted_iota(jnp.int32, (1, LANES), 1).astype(F32)
    logits = jnp.where(lane < N_EXPERTS, _dot3(h, wr_ref[...]), NEG_INF)
    e = jnp.exp(logits - jnp.max(logits, axis=-1, keepdims=True))
    p = e * (1.0 / jnp.sum(e, axis=-1, keepdims=True))
    p = jnp.where(lane < N_EXPERTS, p, -1.0)
    m1 = jnp.max(p, axis=-1, keepdims=True)
    i1 = jnp.min(jnp.where(p == m1, lane, float(LANES)), axis=-1, keepdims=True)
    s1 = lane == i1
    p2 = jnp.where(s1, -1.0, p)
    m2 = jnp.max(p2, axis=-1, keepdims=True)
    i2 = jnp.min(jnp.where(p2 == m2, lane, float(LANES)), axis=-1, keepdims=True)
    s2 = lane == i2
    inv = 1.0 / (m1 + m2)
    gate_ref[0] = jnp.where(s1, m1 * inv, 0.0) + jnp.where(s2, m2 * inv, 0.0)
    sel_ref[0] = jnp.where(s1, 1.0, 0.0) + jnp.where(s2, 1.0, 0.0)


def _router(x, mod_l, ng, wr):
    b, t, _ = x.shape
    tm = min(ROW_TILE, t)
    row = lambda c: pl.BlockSpec((1, tm, c), lambda i, j: (i, j, 0))
    return pl.pallas_call(
        _router_kernel,
        grid=(b, t // tm),
        in_specs=[row(D_MODEL), pl.BlockSpec((1, 6, D_MODEL), lambda i, j: (i, 0, 0)),
                  _const_spec(ng.shape), _const_spec(wr.shape)],
        out_specs=[pl.BlockSpec((2, 1, tm, PLANE_W), lambda i, j: (0, i, j, 0)), row(LANES), row(LANES)],
        out_shape=[jax.ShapeDtypeStruct((2, b, t, PLANE_W), jnp.uint32),
                   jax.ShapeDtypeStruct((b, t, LANES), F32),
                   jax.ShapeDtypeStruct((b, t, LANES), F32)],
        compiler_params=_cparams("parallel", "parallel"),
        name="moe_router",
    )(x, mod_l, ng, wr)


SC_WINDOW = 128
SC_COLS = PLANE_W


def _sc_mesh():
    return plsc.VectorSubcoreMesh(core_axis_name="core", subcore_axis_name="subcore")


def _scatter_rows_sc(src, pos_a, pos_b, n_out):
    n = src.shape[0]

    @pl.kernel(out_type=jax.ShapeDtypeStruct((n_out, SC_COLS), src.dtype), mesh=_sc_mesh(),
               scratch_types=[])
    def scatter(src_hbm, ia_hbm, ib_hbm, dst_hbm):
        def body(rows_vmem, ia_vmem, ib_vmem):
            pltpu.sync_copy(rows_vmem, dst_hbm.at[ia_vmem.at[0]])
            pltpu.sync_copy(rows_vmem, dst_hbm.at[ib_vmem.at[0]])

        idx_spec = pl.BlockSpec((1, SC_WINDOW), lambda i: (0, i))
        pltpu.emit_pipeline(
            body,
            grid=(n // SC_WINDOW,),
            in_specs=[pl.BlockSpec((SC_WINDOW, SC_COLS), lambda i: (i, 0)), idx_spec, idx_spec],
            out_specs=[],
            core_axis_name=("core", "subcore"),
            dimension_semantics=(pltpu.PARALLEL,),
        )(src_hbm, ia_hbm, ib_hbm)

    return scatter(src, pos_a.reshape(1, n), pos_b.reshape(1, n))


def _gather_rows_sc(src, pos):
    n = pos.shape[0]

    @pl.kernel(out_type=jax.ShapeDtypeStruct((n, SC_COLS), src.dtype), mesh=_sc_mesh(),
               scratch_types=[])
    def gather(src_hbm, i_hbm, dst_hbm):
        def body(i_vmem, rows_vmem):
            pltpu.sync_copy(src_hbm.at[i_vmem.at[0]], rows_vmem)

        pltpu.emit_pipeline(
            body,
            grid=(n // SC_WINDOW,),
            in_specs=[pl.BlockSpec((1, SC_WINDOW), lambda i: (0, i))],
            out_specs=[pl.BlockSpec((SC_WINDOW, SC_COLS), lambda i: (i, 0))],
            core_axis_name=("core", "subcore"),
            dimension_semantics=(pltpu.PARALLEL,),
        )(i_hbm, dst_hbm)

    return gather(src, pos.reshape(1, n))


def _expert_kernel(te_ref, nu_ref, x0_ref, x1_ref, wg_ref, wu_ref, wd_ref, ys_ref, xb_ref, acc_ref):
    i, j = pl.program_id(0), pl.program_id(1)
    used = i < nu_ref[0]

    @pl.when(jnp.logical_and(used, j == 0))
    def _():
        for pi, x_ref in enumerate((x0_ref, x1_ref)):
            lo, hi = _unpack_row(x_ref[0])
            xb_ref[:, pi * PLANE_W:(pi + 1) * PLANE_W] = lo.astype(BF16)
            xb_ref[:, HALF_D + pi * PLANE_W:HALF_D + (pi + 1) * PLANE_W] = hi.astype(BF16)
        acc_ref[...] = jnp.zeros_like(acc_ref)

    @pl.when(used)
    def _():
        xb = xb_ref[...]
        g = _dot(xb, wg_ref[0])
        u = _dot(xb, wu_ref[0])
        a = (g * _sigmoid(g) * u).astype(BF16)
        acc_ref[...] += _dot(a, wd_ref[0])

    last = j == pl.num_programs(1) - 1

    @pl.when(jnp.logical_and(used, last))
    def _():
        w = _pack_row(acc_ref[...])
        ys_ref[0] = w[:, :PLANE_W]
        ys_ref[1] = w[:, PLANE_W:]

    @pl.when(jnp.logical_and(jnp.logical_not(used), last))
    def _():
        ys_ref[...] = jnp.zeros_like(ys_ref)


def _experts(xs, tile_expert, n_used, w_up, w_down):
    p = xs.shape[1]
    nck = FF_EXPERT // MOE_FF_CHUNK
    tm = MOE_ROW_TILE

    def chunk(i, j, te, nu):
        return jnp.where(i < nu[0], j, nck - 1)

    grid_spec = pltpu.PrefetchScalarGridSpec(
        num_scalar_prefetch=2,
        grid=(p // tm, nck),
        in_specs=[
            pl.BlockSpec((1, tm, PLANE_W), lambda i, j, te, nu: (0, i, 0)),
            pl.BlockSpec((1, tm, PLANE_W), lambda i, j, te, nu: (1, i, 0)),
            pl.BlockSpec((1, D_MODEL, MOE_FF_CHUNK), lambda i, j, te, nu: (te[i], 0, chunk(i, j, te, nu))),
            pl.BlockSpec((1, D_MODEL, MOE_FF_CHUNK),
                         lambda i, j, te, nu: (te[i], 0, nck + chunk(i, j, te, nu))),
            pl.BlockSpec((1, MOE_FF_CHUNK, D_MODEL), lambda i, j, te, nu: (te[i], chunk(i, j, te, nu), 0)),
        ],
        out_specs=pl.BlockSpec((2, tm, PLANE_W), lambda i, j, te, nu: (0, i, 0)),
        scratch_shapes=[pltpu.VMEM((tm, D_MODEL), BF16), pltpu.VMEM((tm, D_MODEL), F32)],
    )
    return pl.pallas_call(
        _expert_kernel,
        grid_spec=grid_spec,
        out_shape=jax.ShapeDtypeStruct((2, p, PLANE_W), jnp.uint32),
        compiler_params=_cparams("arbitrary", "arbitrary"),
        name="moe_experts",
    )(tile_expert, n_used, xs, xs, w_up, w_up, w_down)


def _combine_kernel(x_ref, mod_ref, g_ref, a0_ref, b0_ref, a1_ref, b1_ref, o_ref):
    g = g_ref[0]
    ga, gb = g[:, 0:1], g[:, 1:2]
    for pi, (a_ref, b_ref) in enumerate(((a0_ref, b0_ref), (a1_ref, b1_ref))):
        a_lo, a_hi = _unpack_row(a_ref[0, 0, 0])
        b_lo, b_hi = _unpack_row(b_ref[0, 0, 0])
        for c0, ff in ((pi * PLANE_W, ga * a_lo + gb * b_lo),
                       (HALF_D + pi * PLANE_W, ga * a_hi + gb * b_hi)):
            cols = slice(c0, c0 + PLANE_W)
            o_ref[0, :, cols] = x_ref[0, :, cols] + mod_ref[0, 5:6, cols] * ff


def _combine(x, mod_l, gate_ab, yg):
    b, t, _ = x.shape
    tm = min(ROW_TILE, t)
    row = pl.BlockSpec((1, tm, D_MODEL), lambda i, j: (i, j, 0))
    part = lambda pi, ab: pl.BlockSpec((1, 1, 1, tm, PLANE_W), lambda i, j: (pi, ab, i, j, 0))
    return pl.pallas_call(
        _combine_kernel,
        grid=(b, t // tm),
        in_specs=[row, pl.BlockSpec((1, 6, D_MODEL), lambda i, j: (i, 0, 0)),
                  pl.BlockSpec((1, tm, 2), lambda i, j: (i, j, 0)),
                  part(0, 0), part(0, 1), part(1, 0), part(1, 1)],
        out_specs=row,
        out_shape=jax.ShapeDtypeStruct(x.shape, F32),
        compiler_params=_cparams("parallel", "parallel"),
        name="moe_combine",
    )(x, mod_l, gate_ab, yg, yg, yg, yg)


def _moe(x, mod_l, ng, wr, w_up, w_down):
    b, t, _ = x.shape
    n = b * t
    h, gates, sel = _router(x, mod_l, ng, wr)
    gates = gates.reshape(n, LANES)[:, :N_EXPERTS]
    sel = sel.reshape(n, LANES)[:, :N_EXPERTS] > 0.5

    tm = MOE_ROW_TILE
    n_tiles = 2 * n // tm + N_EXPERTS
    p = n_tiles * tm
    sel_i = sel.astype(jnp.int32)
    rank = jnp.cumsum(sel_i, axis=0) - sel_i
    counts = jnp.sum(sel_i, axis=0)
    padded = (counts + tm - 1) // tm * tm
    ends = jnp.cumsum(padded)
    pos = (ends - padded)[None, :] + rank
    tile_start = jnp.arange(n_tiles, dtype=jnp.int32) * tm
    tile_expert = jnp.minimum(jnp.sum((tile_start[:, None] >= ends[None, :]).astype(jnp.int32), axis=1),
                              N_EXPERTS - 1)
    n_used = (ends[-1] // tm).astype(jnp.int32).reshape(1)
    order = jnp.cumsum(sel_i, axis=1)
    first, second = sel & (order == 1), sel & (order == 2)
    pos_a = jnp.sum(jnp.where(first, pos, 0), axis=1)
    pos_b = jnp.sum(jnp.where(second, pos, 0), axis=1)
    gate_ab = jnp.stack([jnp.sum(jnp.where(first, gates, 0.0), axis=1),
                         jnp.sum(jnp.where(second, gates, 0.0), axis=1)], axis=1)

    xs = _scatter_rows_sc(h.reshape(2 * n, PLANE_W), jnp.concatenate([pos_a, pos_a + p]),
                          jnp.concatenate([pos_b, pos_b + p]), 2 * p)
    ys = _experts(xs.reshape(2, p, PLANE_W), tile_expert, n_used, w_up, w_down)
    yg = _gather_rows_sc(ys.reshape(2 * p, PLANE_W),
                         jnp.concatenate([pos_a, pos_b, pos_a + p, pos_b + p]))
    return _combine(x, mod_l, gate_ab.reshape(b, t, 2), yg.reshape(2, 2, b, t, PLANE_W))


def _rope_tables(t, dim):
    inv = ROPE_THETA ** (-jnp.arange(0, dim, 2, dtype=F32) / dim)
    ang = jnp.arange(t, dtype=F32)[:, None] * inv[None, :]
    return jnp.cos(ang), jnp.sin(ang)


def _rope_lane_tables(cos, sin, gain, offs, scale):
    t, hw = cos.shape
    ones, zeros = jnp.ones((t, LANES), F32), jnp.zeros((t, LANES), F32)
    c, sa, sb = ones, zeros, zeros
    g_up, g_dn = jnp.roll(gain, hw), jnp.roll(gain, -hw)
    for o in offs:
        c = c.at[:, o:o + hw].set(cos).at[:, o + hw:o + 2 * hw].set(cos)
        sb = sb.at[:, o:o + hw].set(-sin)
        sa = sa.at[:, o + hw:o + 2 * hw].set(sin)
    return jnp.stack([c * gain, sa * g_up, sb * g_dn]) * scale


def _pad_lanes(v, width=LANES):
    return jnp.zeros((width,), F32).at[:v.shape[0]].set(v)


def _even_weights(w_in, qlg, q_up, kvlg, kv_up):
    kr = jnp.zeros((D_MODEL, LANES), F32).at[:, MLA_NOPE:MLA_QK].set(w_in[:, KR_OFF:KR_OFF + MLA_ROPE])
    w_in_p = jnp.concatenate([w_in[:, :KR_OFF], kr, w_in[:, KR_OFF + MLA_ROPE:]], axis=1)
    q_up_p = jnp.pad(q_up.reshape(MLA_Q_RANK, MLA_HEADS, MLA_QK),
                     ((0, 0), (0, 0), (0, LANES - MLA_QK))).reshape(MLA_Q_RANK, MLA_HEADS * LANES)
    kv3 = kv_up.reshape(MLA_KV_RANK, MLA_HEADS, MLA_NOPE + MLA_V)
    k_p = jnp.pad(kv3[:, :, :MLA_NOPE], ((0, 0), (0, 0), (0, LANES - MLA_NOPE)))
    kv_up_p = jnp.concatenate([k_p.reshape(MLA_KV_RANK, -1),
                               kv3[:, :, MLA_NOPE:].reshape(MLA_KV_RANK, -1)], axis=1)
    return (w_in_p.astype(BF16), qlg.reshape(1, -1), q_up_p.astype(BF16), kvlg.reshape(1, -1),
            kv_up_p.astype(BF16))


def _odd_weights(w_in):
    nq = WIN_Q_HEADS * HEAD_DIM
    nkv = WIN_KV_HEADS * HEAD_DIM
    dup = lambda w: jnp.concatenate([w.reshape(D_MODEL, WIN_KV_HEADS, 1, HEAD_DIM)] * 2,
                                    axis=2).reshape(D_MODEL, 2 * nkv)
    return jnp.concatenate([w_in[:, :nq], dup(w_in[:, nq:nq + nkv]), dup(w_in[:, nq + nkv:])],
                           axis=1).astype(BF16)


def _trunk(x, c, p):
    b, t, _ = x.shape
    mod = _ada_mod(c, p["ada_w"], p["ada_b"])
    cos_r, sin_r = _rope_tables(t, MLA_ROPE)
    cos_h, sin_h = _rope_tables(t, HEAD_DIM)
    pair_offs = (0, HEAD_DIM)
    for l in range(DEPTH):
        i = l // 2
        mod_l = mod[l]
        ng = p["norm_g"][l]
        if l % 2 == 0:
            g_q, g_k = p["even_mla_qk_g"][i, 0], p["even_mla_qk_g"][i, 1]
            g_qb, g_kb = p["even_dil_qk_g"][i, 0], p["even_dil_qk_g"][i, 1]
            tabs = (
                _rope_lane_tables(cos_r, sin_r, _pad_lanes(g_q), (MLA_NOPE,),
                                  MLA_QK ** -0.5 * math.log2(math.e)),
                _rope_lane_tables(cos_r, sin_r, _pad_lanes(g_k).at[:MLA_NOPE].set(0.0), (MLA_NOPE,), 1.0),
                _rope_lane_tables(cos_h, sin_h, jnp.tile(g_qb, 2), pair_offs,
                                  HEAD_DIM ** -0.5 * math.log2(math.e)),
                _rope_lane_tables(cos_h, sin_h, jnp.tile(g_kb, 2), pair_offs, 1.0),
            )
            gkn = _pad_lanes(g_k[:MLA_NOPE]).reshape(1, LANES)
            wts = _even_weights(p["even_w_in"][i], p["even_q_latent_g"][i], p["even_q_up"][i],
                                p["even_kv_latent_g"][i], p["even_kv_up"][i]) + (gkn,)
            qa, ka, va, qb, kb, vb = _even_in(x, mod_l, ng[0:1], wts, tabs)
            oa = _mla_attention(qa, ka, va)
            ob = _dil_attention(qb, kb, vb)
            x = _even_out(x, mod_l, oa, ob, p["even_w_out"][i].astype(BF16))
            x = _dense_ffn(x, mod_l, ng[1:2], p["dense_w_up"][i].astype(BF16),
                           p["dense_w_down"][i].astype(BF16))
        else:
            g_q, g_k = p["odd_qk_g"][i, 0], p["odd_qk_g"][i, 1]
            tabs = (
                _rope_lane_tables(cos_h, sin_h, jnp.tile(g_q, 2), pair_offs,
                                  HEAD_DIM ** -0.5 * math.log2(math.e)),
                _rope_lane_tables(cos_h, sin_h, jnp.tile(g_k, 2), pair_offs, 1.0),
            )
            q, k, v = _odd_in(x, mod_l, ng[0:1], _odd_weights(p["odd_w_in"][i]), tabs)
            sink = jnp.repeat(p["odd_sink"][i], HEAD_DIM).reshape(1, -1)
            o = _win_attention(q, k, v, sink)
            x = _odd_out(x, mod_l, o, p["odd_w_out"][i].astype(BF16))
            wr = jnp.zeros((D_MODEL, LANES), F32).at[:, :N_EXPERTS].set(p["moe_router"][i])
            x = _moe(x, mod_l, ng[1:2], wr, p["moe_w_up"][i].astype(BF16),
                     p["moe_w_down"][i].astype(BF16))
    return x


def kernel(x_prompt, x_sample, c_prompt, c_sample, ada_w, ada_b, norm_g, even_w_in, even_q_latent_g, even_q_up, even_kv_latent_g, even_kv_up, even_mla_qk_g, even_dil_qk_g, even_w_out, odd_w_in, odd_qk_g, odd_sink, odd_w_out, dense_w_up, dense_w_down, moe_router, moe_w_up, moe_w_down):
    p = dict(ada_w=ada_w, ada_b=ada_b, norm_g=norm_g, even_w_in=even_w_in,
             even_q_latent_g=even_q_latent_g, even_q_up=even_q_up,
             even_kv_latent_g=even_kv_latent_g, even_kv_up=even_kv_up,
             even_mla_qk_g=even_mla_qk_g, even_dil_qk_g=even_dil_qk_g, even_w_out=even_w_out,
             odd_w_in=odd_w_in, odd_qk_g=odd_qk_g, odd_sink=odd_sink, odd_w_out=odd_w_out,
             dense_w_up=dense_w_up, dense_w_down=dense_w_down, moe_router=moe_router,
             moe_w_up=moe_w_up, moe_w_down=moe_w_down)
    return _trunk(x_prompt, c_prompt, p), _trunk(x_sample, c_sample, p)
```

```python
import functools
import math

import jax
import jax.numpy as jnp
from jax import lax
from jax.experimental import pallas as pl
from jax.experimental.pallas import tpu as pltpu
from jax.experimental.pallas import tpu_sc as plsc

D_MODEL = 1024
DEPTH = 4
RMS_EPS = 1e-6
ROPE_THETA = 10000.0
NEG_INF = -1e30
HEAD_DIM = 64
MLA_HEADS = 8
MLA_NOPE = 64
MLA_ROPE = 32
MLA_QK = MLA_NOPE + MLA_ROPE
MLA_V = 64
MLA_Q_RANK = 384
MLA_KV_RANK = 256
DIL_HEADS = 8
DIL_PATTERNS = ((128, 1), (512, 4), (2048, 16))
WIN_Q_HEADS = 16
WIN_KV_HEADS = 4
WIN_HALF = 128
FF_DENSE = 2816
N_EXPERTS = 8
FF_EXPERT = 3584

LANES = 128
KR_OFF = MLA_Q_RANK + MLA_KV_RANK
DIL_OFF = KR_OFF + 2 * LANES
DIL_W = DIL_HEADS * HEAD_DIM

ROW_TILE = 512
MLA_Q_TILE = 256
BAND_Q_TILE = 128
MOE_ROW_TILE = 1024
MOE_FF_CHUNK = 512
DENSE_FF_CHUNK = 1408
VMEM_LIMIT = 56 * 1024 * 1024

BF16 = jnp.bfloat16
F32 = jnp.float32


def _cparams(*sem):
    return pltpu.CompilerParams(dimension_semantics=sem, vmem_limit_bytes=VMEM_LIMIT)


def _const_spec(shape):
    nd = len(shape)
    return pl.BlockSpec(shape, lambda *_: (0,) * nd, pipeline_mode=pl.Buffered(1))


def _dot(a, b):
    return jnp.dot(a, b, preferred_element_type=F32)


def _dot_t(a, b):
    return lax.dot_general(a, b, (((1,), (1,)), ((), ())), preferred_element_type=F32)


def _split_bf16(a):
    hi = a.astype(BF16)
    lo = (a - hi.astype(F32)).astype(BF16)
    return hi, lo


def _dot3(a, b):
    ah, al = _split_bf16(a)
    bh, bl = _split_bf16(b)
    return _dot(ah, bh) + (_dot(ah, bl) + _dot(al, bh))


def _sigmoid(x):
    return 1.0 / (1.0 + jnp.exp(-x))


HALF_D = D_MODEL // 2
HI16 = 0xFFFF0000
PLANE_W = HALF_D // 2


def _pack_row(x):
    bits = lambda v: lax.bitcast_convert_type(v.astype(BF16).astype(F32), jnp.uint32)
    return (bits(x[:, :HALF_D]) >> 16) | (bits(x[:, HALF_D:]) & jnp.uint32(HI16))


def _unpack_row(w):
    return (lax.bitcast_convert_type(w << 16, F32),
            lax.bitcast_convert_type(w & jnp.uint32(HI16), F32))


def _norm_mod(x, g, sc, sh):
    ms = jnp.mean(x * x, axis=-1, keepdims=True)
    return (x * lax.rsqrt(ms + RMS_EPS) * g) * (1.0 + sc) + sh


def _rms(x, g):
    ms = jnp.mean(x * x, axis=-1, keepdims=True)
    return x * lax.rsqrt(ms + RMS_EPS) * g


def _lane_lo(shape=(1, LANES)):
    return lax.broadcasted_iota(jnp.int32, shape, len(shape) - 1) < HEAD_DIM


def _rope_slab(slab, partner, tab):
    return slab * tab[0] + partner * tab[1]


def _norm_rope_pair(slab, partner, tab):
    lo = _lane_lo()
    sq = slab * slab
    ss_all = jnp.sum(sq, axis=-1, keepdims=True)
    ss_lo = jnp.sum(jnp.where(lo, sq, 0.0), axis=-1, keepdims=True)
    r_lo = lax.rsqrt(ss_lo * (1.0 / HEAD_DIM) + RMS_EPS)
    r_hi = lax.rsqrt((ss_all - ss_lo) * (1.0 / HEAD_DIM) + RMS_EPS)
    return _rope_slab(slab, partner, tab) * jnp.where(lo, r_lo, r_hi)


def _ada_kernel(c_ref, w_ref, b_ref, o_ref):
    c = c_ref[...]
    o_ref[0] = _dot3(c * _sigmoid(c), w_ref[0]) + b_ref[0]


def _ada_mod(c, ada_w, ada_b):
    bc = c.shape[0]
    nchunk = ada_w.shape[2] // D_MODEL
    out = pl.pallas_call(
        _ada_kernel,
        grid=(DEPTH, nchunk),
        in_specs=[
            pl.BlockSpec((bc, D_MODEL), lambda l, j: (0, 0)),
            pl.BlockSpec((1, D_MODEL, D_MODEL), lambda l, j: (l, 0, j)),
            pl.BlockSpec((1, 1, D_MODEL), lambda l, j: (l, 0, j)),
        ],
        out_specs=pl.BlockSpec((1, bc, D_MODEL), lambda l, j: (l, 0, j)),
        out_shape=jax.ShapeDtypeStruct((DEPTH, bc, nchunk * D_MODEL), F32),
        compiler_params=_cparams("parallel", "parallel"),
        name="ada_mod",
    )(c, ada_w, ada_b.reshape(DEPTH, 1, -1))
    return out.reshape(DEPTH, bc, nchunk, D_MODEL)


def _even_in_kernel(x_ref, mod_ref, ng_ref, w_in_ref, qlg_ref, q_up_ref, kvlg_ref, kv_up_ref,
                    gkn_ref, tq_ref, tk_ref, tqb_ref, tkb_ref,
                    qa_ref, ka_ref, va_ref, qb_ref, kb_ref, vb_ref):
    h = _norm_mod(x_ref[0], ng_ref[...], mod_ref[0, 1:2, :], mod_ref[0, 0:1, :]).astype(BF16)
    proj = _dot(h, w_in_ref[...])

    cq = _rms(proj[:, :MLA_Q_RANK], qlg_ref[...]).astype(BF16)
    qraw = _dot(cq, q_up_ref[...])
    tq = (tq_ref[0], tq_ref[1])
    for hd in range(MLA_HEADS):
        slab = qraw[:, hd * LANES:(hd + 1) * LANES]
        partner = qraw[:, (MLA_HEADS + hd) * LANES:(MLA_HEADS + hd + 1) * LANES]
        ss = jnp.sum(slab * slab, axis=-1, keepdims=True)
        rinv = lax.rsqrt(ss * (1.0 / MLA_QK) + RMS_EPS)
        qa_ref[0, :, hd * LANES:(hd + 1) * LANES] = (
            _rope_slab(slab, partner, tq) * rinv).astype(BF16)

    ckv = _rms(proj[:, MLA_Q_RANK:KR_OFF], kvlg_ref[...]).astype(BF16)
    kvraw = _dot(ckv, kv_up_ref[...])
    kr = proj[:, KR_OFF:KR_OFF + LANES]
    ss_kr = jnp.sum(kr * kr, axis=-1, keepdims=True)
    kr_rot = _rope_slab(kr, proj[:, KR_OFF + LANES:DIL_OFF], (tk_ref[0], tk_ref[1]))
    gkn = gkn_ref[...]
    for hd in range(MLA_HEADS):
        slab = kvraw[:, hd * LANES:(hd + 1) * LANES]
        ss = jnp.sum(slab * slab, axis=-1, keepdims=True) + ss_kr
        rinv = lax.rsqrt(ss * (1.0 / MLA_QK) + RMS_EPS)
        ka_ref[0, :, hd * LANES:(hd + 1) * LANES] = ((slab * gkn + kr_rot) * rinv).astype(BF16)
    va_ref[0] = kvraw[:, MLA_HEADS * LANES:].astype(BF16)

    tqb = (tqb_ref[0], tqb_ref[1])
    tkb = (tkb_ref[0], tkb_ref[1])
    part_off = DIL_OFF + 3 * DIL_W
    for j in range(DIL_W // LANES):
        cq0, ck0 = j * LANES, DIL_W + j * LANES
        qb_ref[0, :, j * LANES:(j + 1) * LANES] = _norm_rope_pair(
            proj[:, DIL_OFF + cq0:DIL_OFF + cq0 + LANES],
            proj[:, part_off + cq0:part_off + cq0 + LANES], tqb).astype(BF16)
        kb_ref[0, :, j * LANES:(j + 1) * LANES] = _norm_rope_pair(
            proj[:, DIL_OFF + ck0:DIL_OFF + ck0 + LANES],
            proj[:, part_off + ck0:part_off + ck0 + LANES], tkb).astype(BF16)
    vb_ref[0] = proj[:, DIL_OFF + 2 * DIL_W:DIL_OFF + 3 * DIL_W].astype(BF16)


def _even_in(x, mod_l, ng, wts, tabs):
    b, t, _ = x.shape
    tm = min(ROW_TILE, t)
    w_in, qlg, q_up, kvlg, kv_up, gkn = wts
    row = lambda c: pl.BlockSpec((1, tm, c), lambda i, j: (i, j, 0))
    tab = pl.BlockSpec((2, tm, LANES), lambda i, j: (0, j, 0))
    hc = DIL_HEADS * HEAD_DIM
    outs = [jax.ShapeDtypeStruct((b, t, c), BF16)
            for c in (MLA_HEADS * LANES, MLA_HEADS * LANES, MLA_HEADS * MLA_V, hc, hc, hc)]
    return pl.pallas_call(
        _even_in_kernel,
        grid=(b, t // tm),
        in_specs=[
            row(D_MODEL),
            pl.BlockSpec((1, 6, D_MODEL), lambda i, j: (i, 0, 0)),
            _const_spec(ng.shape), _const_spec(w_in.shape), _const_spec(qlg.shape),
            _const_spec(q_up.shape), _const_spec(kvlg.shape), _const_spec(kv_up.shape),
            _const_spec(gkn.shape), tab, tab, tab, tab,
        ],
        out_specs=[row(s.shape[2]) for s in outs],
        out_shape=outs,
        compiler_params=_cparams("parallel", "parallel"),
        name="even_in",
    )(x, mod_l, ng, w_in, qlg, q_up, kvlg, kv_up, gkn, *tabs)


def _mla_kernel(q_ref, k_ref, v_ref, o_ref):
    lo = _lane_lo()
    v = v_ref[0]
    one = jnp.ones((), BF16)
    v_aug = (jnp.where(lo, v, one), jnp.where(lo, one, v))
    outs = []
    for hd in range(2):
        q = q_ref[0, :, hd * LANES:(hd + 1) * LANES]
        k = k_ref[0, :, hd * LANES:(hd + 1) * LANES]
        s = _dot_t(q, k)
        m = jnp.max(s, axis=-1, keepdims=True)
        p = jnp.exp2((s - m).astype(BF16))
        outs.append(_dot(p, v_aug[hd]))
    num = jnp.where(lo, outs[0], outs[1])
    den = jnp.where(lo, pltpu.roll(outs[0], HEAD_DIM, 1), pltpu.roll(outs[1], HEAD_DIM, 1))
    o_ref[0] = (num * (1.0 / den)).astype(BF16)


def _mla_attention(q, k, v):
    b, t, _ = q.shape
    tq = min(MLA_Q_TILE, t)
    return pl.pallas_call(
        _mla_kernel,
        grid=(b, MLA_HEADS // 2, t // tq),
        in_specs=[
            pl.BlockSpec((1, tq, 2 * LANES), lambda i, c, j: (i, j, c)),
            pl.BlockSpec((1, t, 2 * LANES), lambda i, c, j: (i, 0, c)),
            pl.BlockSpec((1, t, LANES), lambda i, c, j: (i, 0, c)),
        ],
        out_specs=pl.BlockSpec((1, tq, LANES), lambda i, c, j: (i, j, c)),
        out_shape=jax.ShapeDtypeStruct((b, t, MLA_HEADS * MLA_V), BF16),
        compiler_params=_cparams("parallel", "parallel", "parallel"),
        name="mla_attn",
    )(q, k, v)


def _band_chain(qs_list, kw, vw, mask, tq):
    lo = _lane_lo()
    m_lo = jnp.where(lo, 1.0, 0.0).astype(BF16)
    m_hi = jnp.where(lo, 0.0, 1.0).astype(BF16)
    parts = []
    for qs in qs_list:
        parts += [qs * m_lo, qs * m_hi]
    s = mask(_dot_t(jnp.concatenate(parts, axis=0), kw))
    m = jnp.max(s, axis=-1, keepdims=True)
    p = jnp.exp2(s - m)
    l = jnp.sum(p, axis=-1, keepdims=True)
    o = _dot(p.astype(BF16), vw) * (1.0 / l)
    lse = (m + jnp.log2(l)) * math.log(2.0)
    res = []
    for qi in range(len(qs_list)):
        a, b = 2 * qi * tq, (2 * qi + 1) * tq
        res.append((jnp.where(lo, o[a:a + tq], o[b:b + tq]),
                    jnp.where(lo, lse[a:a + tq], lse[b:b + tq])))
    return res


def _band_rel(rows, tq, win, half):
    return (lax.rem(lax.broadcasted_iota(jnp.int32, (rows, win), 0), tq)
            - lax.broadcasted_iota(jnp.int32, (rows, win), 1)) + half


def _band_valid(relh, off, half):
    return lax.bitcast_convert_type(relh + off, jnp.uint32) <= jnp.uint32(2 * half)


def _win_kernel(q_ref, k_ref, v_ref, sink_ref, o_ref, *, seq, tq, win):
    half, nq = WIN_HALF, 2
    relh = _band_rel(2 * nq * tq, tq, win, half)
    bias_mid = jnp.where(_band_valid(relh, half, half), 0.0, NEG_INF)

    def tile(q0, ws, interior):
        if interior:
            mask = lambda s: s + bias_mid
        else:
            valid = _band_valid(relh, q0 - ws, half)
            mask = lambda s: jnp.where(valid, s, NEG_INF)
        qs = [q_ref[0, pl.ds(q0, tq), qi * LANES:(qi + 1) * LANES] for qi in range(nq)]
        res = _band_chain(qs, k_ref[0, pl.ds(ws, win), :], v_ref[0, pl.ds(ws, win), :], mask, tq)
        for qi, (o_slab, lse_slab) in enumerate(res):
            keep = _sigmoid(lse_slab - sink_ref[:, qi * LANES:(qi + 1) * LANES])
            o_ref[0, pl.ds(q0, tq), qi * LANES:(qi + 1) * LANES] = (o_slab * keep).astype(BF16)

    n_t = seq // tq
    tile(0, 0, False)
    if n_t > 1:
        tile(seq - tq, seq - win, False)

    def mid(ti, carry):
        q0 = pl.multiple_of(ti * tq, tq)
        tile(q0, pl.multiple_of(q0 - half, HEAD_DIM), True)
        return carry

    if n_t > 2:
        lax.fori_loop(1, n_t - 1, mid, 0, unroll=2)


def _win_attention(q, k, v, sink):
    b, seq, _ = q.shape
    tq = min(BAND_Q_TILE, seq)
    win = min(tq + 2 * WIN_HALF, seq)
    assert tq >= WIN_HALF or win == seq
    kern = functools.partial(_win_kernel, seq=seq, tq=tq, win=win)
    return pl.pallas_call(
        kern,
        grid=(b, WIN_KV_HEADS),
        in_specs=[
            pl.BlockSpec((1, seq, 2 * LANES), lambda i, c: (i, 0, c)),
            pl.BlockSpec((1, seq, LANES), lambda i, c: (i, 0, c)),
            pl.BlockSpec((1, seq, LANES), lambda i, c: (i, 0, c)),
            pl.BlockSpec((1, 2 * LANES), lambda i, c: (0, c)),
        ],
        out_specs=pl.BlockSpec((1, seq, 2 * LANES), lambda i, c: (i, 0, c)),
        out_shape=jax.ShapeDtypeStruct(q.shape, BF16),
        compiler_params=_cparams("parallel", "parallel"),
        name="win_attn",
    )(q, k, v, sink)


def _dil_kernel(q_ref, k_ref, v_ref, o_ref, qf, kf, vf, qd, kd, vd, acc, lse, of, *, seq):
    qf[...] = q_ref[0].astype(F32)
    kf[...] = k_ref[0].astype(F32)
    vf[...] = v_ref[0].astype(F32)
    for pi in (1, 2):
        d = DIL_PATTERNS[pi][1]
        ln = seq // d
        for r in range(d):
            for src, dst in ((qf, qd), (kf, kd), (vf, vd)):
                dst[pi - 1, r * ln:(r + 1) * ln, :] = src[pl.ds(r, ln, stride=d), :].astype(BF16)

    for pi, (window, d) in enumerate(DIL_PATTERNS):
        ln = seq // d
        half = window // (2 * d)
        tq = min(BAND_Q_TILE, ln)
        win = min(tq + 2 * half, ln)
        n_t = ln // tq
        align = min(tq, HEAD_DIM)
        rel = _band_rel(2 * tq, tq, win, half)
        if pi == 0:
            load = lambda ref, row, n: ref[0, pl.ds(row, n), :]
            srcs = (q_ref, k_ref, v_ref)
        else:
            load = lambda ref, row, n, pi=pi: ref[pi - 1, pl.ds(row, n), :]
            srcs = (qd, kd, vd)

        def body(i, carry, pi=pi, ln=ln, half=half, tq=tq, win=win, n_t=n_t, align=align,
                 rel=rel, load=load, srcs=srcs):
            r = i // n_t
            q0 = (i - r * n_t) * tq
            ws = 0 if win == ln else jnp.clip(q0 - half, 0, ln - win)
            valid = _band_valid(rel, q0 - ws, half)
            rq = pl.multiple_of(r * ln + q0, align)
            rk = pl.multiple_of(r * ln + ws, align)
            ((o_slab, lse_slab),) = _band_chain(
                [load(srcs[0], rq, tq)], load(srcs[1], rk, win), load(srcs[2], rk, win),
                lambda s: jnp.where(valid, s, NEG_INF), tq)
            acc[pi, pl.ds(rq, tq), :] = o_slab
            lse[pi, pl.ds(rq, tq), :] = lse_slab
            return carry

        lax.fori_loop(0, seq // tq, body, 0, unroll=min(4, seq // tq))

    d2, d3 = DIL_PATTERNS[1][1], DIL_PATTERNS[2][1]
    l2n, l3n = seq // d2, seq // d3
    for r in range(d3):
        s1 = pl.ds(r, l3n, stride=d3)
        s2 = pl.ds((r % d2) * l2n + r // d2, l3n, stride=d3 // d2)
        s3 = pl.ds(r * l3n, l3n)
        l1, l2, l3 = lse[0, s1, :], lse[1, s2, :], lse[2, s3, :]
        m = jnp.maximum(jnp.maximum(l1, l2), l3)
        e1, e2, e3 = jnp.exp(l1 - m), jnp.exp(l2 - m), jnp.exp(l3 - m)
        of[s1, :] = ((e1 * acc[0, s1, :] + e2 * acc[1, s2, :] + e3 * acc[2, s3, :])
                     * (1.0 / (e1 + e2 + e3)))
    o_ref[0] = of[...].astype(BF16)


def _dil_attention(q, k, v):
    b, seq, c = q.shape
    blk = pl.BlockSpec((1, seq, LANES), lambda i, j: (i, 0, j))
    f32s = pltpu.VMEM((seq, LANES), F32)
    res = pltpu.VMEM((2, seq, LANES), BF16)
    pat = pltpu.VMEM((len(DIL_PATTERNS), seq, LANES), F32)
    return pl.pallas_call(
        functools.partial(_dil_kernel, seq=seq),
        grid=(b, c // LANES),
        in_specs=[blk, blk, blk],
        out_specs=blk,
        out_shape=jax.ShapeDtypeStruct(q.shape, BF16),
        scratch_shapes=[f32s, f32s, f32s, res, res, res, pat, pat, f32s],
        compiler_params=_cparams("parallel", "parallel"),
        name="dil_attn",
    )(q, k, v)


def _even_out_kernel(x_ref, mod_ref, oa_ref, ob_ref, w_ref, y_ref):
    half = w_ref.shape[0] // 2
    mix = _dot(oa_ref[0], w_ref[:half, :]) + _dot(ob_ref[0], w_ref[half:, :])
    y_ref[0] = x_ref[0] + mod_ref[0, 2:3, :] * mix


def _even_out(x, mod_l, oa, ob, w_out):
    b, t, _ = x.shape
    tm = min(ROW_TILE, t)
    row = lambda c: pl.BlockSpec((1, tm, c), lambda i, j: (i, j, 0))
    return pl.pallas_call(
        _even_out_kernel,
        grid=(b, t // tm),
        in_specs=[row(D_MODEL), pl.BlockSpec((1, 6, D_MODEL), lambda i, j: (i, 0, 0)),
                  row(oa.shape[2]), row(ob.shape[2]), _const_spec(w_out.shape)],
        out_specs=row(D_MODEL),
        out_shape=jax.ShapeDtypeStruct(x.shape, F32),
        compiler_params=_cparams("parallel", "parallel"),
        name="even_out",
    )(x, mod_l, oa, ob, w_out)


def _dense_ffn_kernel(x_ref, mod_ref, ng_ref, wu_ref, wd_ref, y_ref):
    x = x_ref[0]
    h = _norm_mod(x, ng_ref[...], mod_ref[0, 4:5, :], mod_ref[0, 3:4, :]).astype(BF16)
    ff = wd_ref.shape[0]
    acc = None
    for c0 in range(0, ff, DENSE_FF_CHUNK):
        g = _dot(h, wu_ref[:, c0:c0 + DENSE_FF_CHUNK])
        u = _dot(h, wu_ref[:, ff + c0:ff + c0 + DENSE_FF_CHUNK])
        a = (g * _sigmoid(g) * u).astype(BF16)
        d = _dot(a, wd_ref[c0:c0 + DENSE_FF_CHUNK, :])
        acc = d if acc is None else acc + d
    y_ref[0] = x + mod_ref[0, 5:6, :] * acc


def _dense_ffn(x, mod_l, ng, w_up, w_down):
    b, t, _ = x.shape
    tm = min(ROW_TILE, t)
    row = pl.BlockSpec((1, tm, D_MODEL), lambda i, j: (i, j, 0))
    return pl.pallas_call(
        _dense_ffn_kernel,
        grid=(b, t // tm),
        in_specs=[row, pl.BlockSpec((1, 6, D_MODEL), lambda i, j: (i, 0, 0)),
                  _const_spec(ng.shape), _const_spec(w_up.shape), _const_spec(w_down.shape)],
        out_specs=row,
        out_shape=jax.ShapeDtypeStruct(x.shape, F32),
        compiler_params=_cparams("parallel", "parallel"),
        name="dense_ffn",
    )(x, mod_l, ng, w_up, w_down)


def _odd_in_kernel(x_ref, mod_ref, ng_ref, w_in_ref, tq_ref, tk_ref, q_ref, k_ref, v_ref):
    h = _norm_mod(x_ref[0], ng_ref[...], mod_ref[0, 1:2, :], mod_ref[0, 0:1, :]).astype(BF16)
    proj = _dot(h, w_in_ref[...])
    tq = (tq_ref[0], tq_ref[1])
    tk = (tk_ref[0], tk_ref[1])
    nq = WIN_Q_HEADS * HEAD_DIM // LANES
    part = (nq + 2 * WIN_KV_HEADS) * LANES
    for j in range(nq):
        q_ref[0, :, j * LANES:(j + 1) * LANES] = _norm_rope_pair(
            proj[:, j * LANES:(j + 1) * LANES],
            proj[:, part + j * LANES:part + (j + 1) * LANES], tq).astype(BF16)
    for j in range(WIN_KV_HEADS):
        c0 = (nq + j) * LANES
        k_ref[0, :, j * LANES:(j + 1) * LANES] = _norm_rope_pair(
            proj[:, c0:c0 + LANES], proj[:, part + c0:part + c0 + LANES], tk).astype(BF16)
    v_ref[0] = proj[:, (nq + WIN_KV_HEADS) * LANES:part].astype(BF16)


def _odd_in(x, mod_l, ng, w_in, tabs):
    b, t, _ = x.shape
    tm = min(ROW_TILE, t)
    row = lambda c: pl.BlockSpec((1, tm, c), lambda i, j: (i, j, 0))
    tab = pl.BlockSpec((2, tm, LANES), lambda i, j: (0, j, 0))
    widths = (WIN_Q_HEADS * HEAD_DIM, WIN_KV_HEADS * LANES, WIN_KV_HEADS * LANES)
    return pl.pallas_call(
        _odd_in_kernel,
        grid=(b, t // tm),
        in_specs=[row(D_MODEL), pl.BlockSpec((1, 6, D_MODEL), lambda i, j: (i, 0, 0)),
                  _const_spec(ng.shape), _const_spec(w_in.shape), tab, tab],
        out_specs=[row(c) for c in widths],
        out_shape=[jax.ShapeDtypeStruct((b, t, c), BF16) for c in widths],
        compiler_params=_cparams("parallel", "parallel"),
        name="odd_in",
    )(x, mod_l, ng, w_in, *tabs)


def _odd_out_kernel(x_ref, mod_ref, o_ref, w_ref, y_ref):
    y_ref[0] = x_ref[0] + mod_ref[0, 2:3, :] * _dot(o_ref[0], w_ref[...])


def _odd_out(x, mod_l, o, w_out):
    b, t, _ = x.shape
    tm = min(ROW_TILE, t)
    row = pl.BlockSpec((1, tm, D_MODEL), lambda i, j: (i, j, 0))
    return pl.pallas_call(
        _odd_out_kernel,
        grid=(b, t // tm),
        in_specs=[row, pl.BlockSpec((1, 6, D_MODEL), lambda i, j: (i, 0, 0)), row,
                  _const_spec(w_out.shape)],
        out_specs=row,
        out_shape=jax.ShapeDtypeStruct(x.shape, F32),
        compiler_params=_cparams("parallel", "parallel"),
        name="odd_out",
    )(x, mod_l, o, w_out)


def _router_kernel(x_ref, mod_ref, ng_ref, wr_ref, h_ref, gate_ref, sel_ref):
    h = _norm_mod(x_ref[0], ng_ref[...], mod_ref[0, 4:5, :], mod_ref[0, 3:4, :])
    w = _pack_row(h)
    h_ref[0, 0] = w[:, :PLANE_W]
    h_ref[1, 0] = w[:, PLANE_W:]
    lane = lax.broadcasted_iota(jnp.int32, (1, LANES), 1).astype(F32)
    logits = jnp.where(lane < N_EXPERTS, _dot3(h, wr_ref[...]), NEG_INF)
    e = jnp.exp(logits - jnp.max(logits, axis=-1, keepdims=True))
    p = e * (1.0 / jnp.sum(e, axis=-1, keepdims=True))
    p = jnp.where(lane < N_EXPERTS, p, -1.0)
    m1 = jnp.max(p, axis=-1, keepdims=True)
    i1 = jnp.min(jnp.where(p == m1, lane, float(LANES)), axis=-1, keepdims=True)
    s1 = lane == i1
    p2 = jnp.where(s1, -1.0, p)
    m2 = jnp.max(p2, axis=-1, keepdims=True)
    i2 = jnp.min(jnp.where(p2 == m2, lane, float(LANES)), axis=-1, keepdims=True)
    s2 = lane == i2
    inv = 1.0 / (m1 + m2)
    gate_ref[0] = jnp.where(s1, m1 * inv, 0.0) + jnp.where(s2, m2 * inv, 0.0)
    sel_ref[0] = jnp.where(s1, 1.0, 0.0) + jnp.where(s2, 1.0, 0.0)


def _router(x, mod_l, ng, wr):
    b, t, _ = x.shape
    tm = min(ROW_TILE, t)
    row = lambda c: pl.BlockSpec((1, tm, c), lambda i, j: (i, j, 0))
    return pl.pallas_call(
        _router_kernel,
        grid=(b, t // tm),
        in_specs=[row(D_MODEL), pl.BlockSpec((1, 6, D_MODEL), lambda i, j: (i, 0, 0)),
                  _const_spec(ng.shape), _const_spec(wr.shape)],
        out_specs=[pl.BlockSpec((2, 1, tm, PLANE_W), lambda i, j: (0, i, j, 0)), row(LANES), row(LANES)],
        out_shape=[jax.ShapeDtypeStruct((2, b, t, PLANE_W), jnp.uint32),
                   jax.ShapeDtypeStruct((b, t, LANES), F32),
                   jax.ShapeDtypeStruct((b, t, LANES), F32)],
        compiler_params=_cparams("parallel", "parallel"),
        name="moe_router",
    )(x, mod_l, ng, wr)


SC_WINDOW = 128
SC_COLS = PLANE_W


def _sc_mesh():
    return plsc.VectorSubcoreMesh(core_axis_name="core", subcore_axis_name="subcore")


def _scatter_rows_sc(src, pos_a, pos_b, n_out):
    n = src.shape[0]

    @pl.kernel(out_type=jax.ShapeDtypeStruct((n_out, SC_COLS), src.dtype), mesh=_sc_mesh(),
               scratch_types=[])
    def scatter(src_hbm, ia_hbm, ib_hbm, dst_hbm):
        def body(rows_vmem, ia_vmem, ib_vmem):
            pltpu.sync_copy(rows_vmem, dst_hbm.at[ia_vmem.at[0]])
            pltpu.sync_copy(rows_vmem, dst_hbm.at[ib_vmem.at[0]])

        idx_spec = pl.BlockSpec((1, SC_WINDOW), lambda i: (0, i))
        pltpu.emit_pipeline(
            body,
            grid=(n // SC_WINDOW,),
            in_specs=[pl.BlockSpec((SC_WINDOW, SC_COLS), lambda i: (i, 0)), idx_spec, idx_spec],
            out_specs=[],
            core_axis_name=("core", "subcore"),
            dimension_semantics=(pltpu.PARALLEL,),
        )(src_hbm, ia_hbm, ib_hbm)

    return scatter(src, pos_a.reshape(1, n), pos_b.reshape(1, n))


def _gather_rows_sc(src, pos):
    n = pos.shape[0]

    @pl.kernel(out_type=jax.ShapeDtypeStruct((n, SC_COLS), src.dtype), mesh=_sc_mesh(),
               scratch_types=[])
    def gather(src_hbm, i_hbm, dst_hbm):
        def body(i_vmem, rows_vmem):
            pltpu.sync_copy(src_hbm.at[i_vmem.at[0]], rows_vmem)

        pltpu.emit_pipeline(
            body,
            grid=(n // SC_WINDOW,),
            in_specs=[pl.BlockSpec((1, SC_WINDOW), lambda i: (0, i))],
            out_specs=[pl.BlockSpec((SC_WINDOW, SC_COLS), lambda i: (i, 0))],
            core_axis_name=("core", "subcore"),
            dimension_semantics=(pltpu.PARALLEL,),
        )(i_hbm, dst_hbm)

    return gather(src, pos.reshape(1, n))


def _expert_kernel(te_ref, nu_ref, x0_ref, x1_ref, wg_ref, wu_ref, wd_ref, ys_ref, xb_ref, acc_ref):
    i, j = pl.program_id(0), pl.program_id(1)
    used = i < nu_ref[0]

    @pl.when(jnp.logical_and(used, j == 0))
    def _():
        for pi, x_ref in enumerate((x0_ref, x1_ref)):
            lo, hi = _unpack_row(x_ref[0])
            xb_ref[:, pi * PLANE_W:(pi + 1) * PLANE_W] = lo.astype(BF16)
            xb_ref[:, HALF_D + pi * PLANE_W:HALF_D + (pi + 1) * PLANE_W] = hi.astype(BF16)
        acc_ref[...] = jnp.zeros_like(acc_ref)

    @pl.when(used)
    def _():
        xb = xb_ref[...]
        g = _dot(xb, wg_ref[0])
        u = _dot(xb, wu_ref[0])
        a = (g * _sigmoid(g) * u).astype(BF16)
        acc_ref[...] += _dot(a, wd_ref[0])

    last = j == pl.num_programs(1) - 1

    @pl.when(jnp.logical_and(used, last))
    def _():
        w = _pack_row(acc_ref[...])
        ys_ref[0] = w[:, :PLANE_W]
        ys_ref[1] = w[:, PLANE_W:]

    @pl.when(jnp.logical_and(jnp.logical_not(used), last))
    def _():
        ys_ref[...] = jnp.zeros_like(ys_ref)


def _experts(xs, tile_expert, n_used, w_up, w_down):
    p = xs.shape[1]
    nck = FF_EXPERT // MOE_FF_CHUNK
    tm = MOE_ROW_TILE

    def chunk(i, j, te, nu):
        return jnp.where(i < nu[0], j, nck - 1)

    grid_spec = pltpu.PrefetchScalarGridSpec(
        num_scalar_prefetch=2,
        grid=(p // tm, nck),
        in_specs=[
            pl.BlockSpec((1, tm, PLANE_W), lambda i, j, te, nu: (0, i, 0)),
            pl.BlockSpec((1, tm, PLANE_W), lambda i, j, te, nu: (1, i, 0)),
            pl.BlockSpec((1, D_MODEL, MOE_FF_CHUNK), lambda i, j, te, nu: (te[i], 0, chunk(i, j, te, nu))),
            pl.BlockSpec((1, D_MODEL, MOE_FF_CHUNK),
                         lambda i, j, te, nu: (te[i], 0, nck + chunk(i, j, te, nu))),
            pl.BlockSpec((1, MOE_FF_CHUNK, D_MODEL), lambda i, j, te, nu: (te[i], chunk(i, j, te, nu), 0)),
        ],
        out_specs=pl.BlockSpec((2, tm, PLANE_W), lambda i, j, te, nu: (0, i, 0)),
        scratch_shapes=[pltpu.VMEM((tm, D_MODEL), BF16), pltpu.VMEM((tm, D_MODEL), F32)],
    )
    return pl.pallas_call(
        _expert_kernel,
        grid_spec=grid_spec,
        out_shape=jax.ShapeDtypeStruct((2, p, PLANE_W), jnp.uint32),
        compiler_params=_cparams("arbitrary", "arbitrary"),
        name="moe_experts",
    )(tile_expert, n_used, xs, xs, w_up, w_up, w_down)


def _combine_kernel(x_ref, mod_ref, g_ref, a0_ref, b0_ref, a1_ref, b1_ref, o_ref):
    g = g_ref[0]
    ga, gb = g[:, 0:1], g[:, 1:2]
    for pi, (a_ref, b_ref) in enumerate(((a0_ref, b0_ref), (a1_ref, b1_ref))):
        a_lo, a_hi = _unpack_row(a_ref[0, 0, 0])
        b_lo, b_hi = _unpack_row(b_ref[0, 0, 0])
        for c0, ff in ((pi * PLANE_W, ga * a_lo + gb * b_lo),
                       (HALF_D + pi * PLANE_W, ga * a_hi + gb * b_hi)):
            cols = slice(c0, c0 + PLANE_W)
            o_ref[0, :, cols] = x_ref[0, :, cols] + mod_ref[0, 5:6, cols] * ff


def _combine(x, mod_l, gate_ab, yg):
    b, t, _ = x.shape
    tm = min(ROW_TILE, t)
    row = pl.BlockSpec((1, tm, D_MODEL), lambda i, j: (i, j, 0))
    part = lambda pi, ab: pl.BlockSpec((1, 1, 1, tm, PLANE_W), lambda i, j: (pi, ab, i, j, 0))
    return pl.pallas_call(
        _combine_kernel,
        grid=(b, t // tm),
        in_specs=[row, pl.BlockSpec((1, 6, D_MODEL), lambda i, j: (i, 0, 0)),
                  pl.BlockSpec((1, tm, 2), lambda i, j: (i, j, 0)),
                  part(0, 0), part(0, 1), part(1, 0), part(1, 1)],
        out_specs=row,
        out_shape=jax.ShapeDtypeStruct(x.shape, F32),
        compiler_params=_cparams("parallel", "parallel"),
        name="moe_combine",
    )(x, mod_l, gate_ab, yg, yg, yg, yg)


def _moe(x, mod_l, ng, wr, w_up, w_down):
    b, t, _ = x.shape
    n = b * t
    h, gates, sel = _router(x, mod_l, ng, wr)
    gates = gates.reshape(n, LANES)[:, :N_EXPERTS]
    sel = sel.reshape(n, LANES)[:, :N_EXPERTS] > 0.5

    tm = MOE_ROW_TILE
    n_tiles = 2 * n // tm + N_EXPERTS
    p = n_tiles * tm
    sel_i = sel.astype(jnp.int32)
    rank = jnp.cumsum(sel_i, axis=0) - sel_i
    counts = jnp.sum(sel_i, axis=0)
    padded = (counts + tm - 1) // tm * tm
    ends = jnp.cumsum(padded)
    pos = (ends - padded)[None, :] + rank
    tile_start = jnp.arange(n_tiles, dtype=jnp.int32) * tm
    tile_expert = jnp.minimum(jnp.sum((tile_start[:, None] >= ends[None, :]).astype(jnp.int32), axis=1),
                              N_EXPERTS - 1)
    n_used = (ends[-1] // tm).astype(jnp.int32).reshape(1)
    order = jnp.cumsum(sel_i, axis=1)
    first, second = sel & (order == 1), sel & (order == 2)
    pos_a = jnp.sum(jnp.where(first, pos, 0), axis=1)
    pos_b = jnp.sum(jnp.where(second, pos, 0), axis=1)
    gate_ab = jnp.stack([jnp.sum(jnp.where(first, gates, 0.0), axis=1),
                         jnp.sum(jnp.where(second, gates, 0.0), axis=1)], axis=1)

    xs = _scatter_rows_sc(h.reshape(2 * n, PLANE_W), jnp.concatenate([pos_a, pos_a + p]),
                          jnp.concatenate([pos_b, pos_b + p]), 2 * p)
    ys = _experts(xs.reshape(2, p, PLANE_W), tile_expert, n_used, w_up, w_down)
    yg = _gather_rows_sc(ys.reshape(2 * p, PLANE_W),
                         jnp.concatenate([pos_a, pos_b, pos_a + p, pos_b + p]))
    return _combine(x, mod_l, gate_ab.reshape(b, t, 2), yg.reshape(2, 2, b, t, PLANE_W))


def _rope_tables(t, dim):
    inv = ROPE_THETA ** (-jnp.arange(0, dim, 2, dtype=F32) / dim)
    ang = jnp.arange(t, dtype=F32)[:, None] * inv[None, :]
    return jnp.cos(ang), jnp.sin(ang)


def _rope_lane_tables(cos, sin, gain, offs, scale):
    t, hw = cos.shape
    c, s = jnp.ones((t, LANES), F32), jnp.zeros((t, LANES), F32)
    g_up, g_dn = jnp.roll(gain, hw), jnp.roll(gain, -hw)
    g_partner = jnp.zeros((LANES,), F32)
    for o in offs:
        c = c.at[:, o:o + hw].set(cos).at[:, o + hw:o + 2 * hw].set(cos)
        s = s.at[:, o:o + hw].set(-sin).at[:, o + hw:o + 2 * hw].set(sin)
        g_partner = g_partner.at[o:o + hw].set(g_dn[o:o + hw]).at[o + hw:o + 2 * hw].set(
            g_up[o + hw:o + 2 * hw])
    return jnp.stack([c * gain, s * g_partner]) * scale


def _swap_halves(w, width):
    d, c = w.shape
    return w.reshape(d, c // width, 2, width // 2)[:, :, ::-1, :].reshape(d, c)


def _pad_lanes(v, width=LANES):
    return jnp.zeros((width,), F32).at[:v.shape[0]].set(v)


def _even_weights(w_in, qlg, q_up, kvlg, kv_up):
    w_kr = w_in[:, KR_OFF:KR_OFF + MLA_ROPE]
    slab = lambda w: jnp.zeros((D_MODEL, LANES), F32).at[:, MLA_NOPE:MLA_QK].set(w)
    w_dil = w_in[:, KR_OFF + MLA_ROPE:]
    n_qk = 2 * DIL_HEADS * HEAD_DIM
    w_in_p = jnp.concatenate([w_in[:, :KR_OFF], slab(w_kr), slab(_swap_halves(w_kr, MLA_ROPE)),
                              w_dil, _swap_halves(w_dil[:, :n_qk], HEAD_DIM)], axis=1)
    q3 = q_up.reshape(MLA_Q_RANK, MLA_HEADS, MLA_QK)
    q_rope_p = _swap_halves(q3[:, :, MLA_NOPE:].reshape(MLA_Q_RANK, -1), MLA_ROPE)
    q_part = jnp.pad(q_rope_p.reshape(MLA_Q_RANK, MLA_HEADS, MLA_ROPE),
                     ((0, 0), (0, 0), (MLA_NOPE, LANES - MLA_QK)))
    q_up_p = jnp.concatenate([jnp.pad(q3, ((0, 0), (0, 0), (0, LANES - MLA_QK))), q_part],
                             axis=1).reshape(MLA_Q_RANK, 2 * MLA_HEADS * LANES)
    kv3 = kv_up.reshape(MLA_KV_RANK, MLA_HEADS, MLA_NOPE + MLA_V)
    k_p = jnp.pad(kv3[:, :, :MLA_NOPE], ((0, 0), (0, 0), (0, LANES - MLA_NOPE)))
    kv_up_p = jnp.concatenate([k_p.reshape(MLA_KV_RANK, -1),
                               kv3[:, :, MLA_NOPE:].reshape(MLA_KV_RANK, -1)], axis=1)
    return (w_in_p.astype(BF16), qlg.reshape(1, -1), q_up_p.astype(BF16), kvlg.reshape(1, -1),
            kv_up_p.astype(BF16))


def _odd_weights(w_in):
    nq = WIN_Q_HEADS * HEAD_DIM
    nkv = WIN_KV_HEADS * HEAD_DIM
    dup = lambda w: jnp.concatenate([w.reshape(D_MODEL, WIN_KV_HEADS, 1, HEAD_DIM)] * 2,
                                    axis=2).reshape(D_MODEL, 2 * nkv)
    w_q, w_k = w_in[:, :nq], dup(w_in[:, nq:nq + nkv])
    return jnp.concatenate([w_q, w_k, dup(w_in[:, nq + nkv:]), _swap_halves(w_q, HEAD_DIM),
                            _swap_halves(w_k, HEAD_DIM)], axis=1).astype(BF16)


def _trunk(x, c, p):
    b, t, _ = x.shape
    mod = _ada_mod(c, p["ada_w"], p["ada_b"])
    cos_r, sin_r = _rope_tables(t, MLA_ROPE)
    cos_h, sin_h = _rope_tables(t, HEAD_DIM)
    pair_offs = (0, HEAD_DIM)
    for l in range(DEPTH):
        i = l // 2
        mod_l = mod[l]
        ng = p["norm_g"][l]
        if l % 2 == 0:
            g_q, g_k = p["even_mla_qk_g"][i, 0], p["even_mla_qk_g"][i, 1]
            g_qb, g_kb = p["even_dil_qk_g"][i, 0], p["even_dil_qk_g"][i, 1]
            tabs = (
                _rope_lane_tables(cos_r, sin_r, _pad_lanes(g_q), (MLA_NOPE,),
                                  MLA_QK ** -0.5 * math.log2(math.e)),
                _rope_lane_tables(cos_r, sin_r, _pad_lanes(g_k).at[:MLA_NOPE].set(0.0), (MLA_NOPE,), 1.0),
                _rope_lane_tables(cos_h, sin_h, jnp.tile(g_qb, 2), pair_offs,
                                  HEAD_DIM ** -0.5 * math.log2(math.e)),
                _rope_lane_tables(cos_h, sin_h, jnp.tile(g_kb, 2), pair_offs, 1.0),
            )
            gkn = _pad_lanes(g_k[:MLA_NOPE]).reshape(1, LANES)
            wts = _even_weights(p["even_w_in"][i], p["even_q_latent_g"][i], p["even_q_up"][i],
                                p["even_kv_latent_g"][i], p["even_kv_up"][i]) + (gkn,)
            qa, ka, va, qb, kb, vb = _even_in(x, mod_l, ng[0:1], wts, tabs)
            oa = _mla_attention(qa, ka, va)
            ob = _dil_attention(qb, kb, vb)
            x = _even_out(x, mod_l, oa, ob, p["even_w_out"][i].astype(BF16))
            x = _dense_ffn(x, mod_l, ng[1:2], p["dense_w_up"][i].astype(BF16),
                           p["dense_w_down"][i].astype(BF16))
        else:
            g_q, g_k = p["odd_qk_g"][i, 0], p["odd_qk_g"][i, 1]
            tabs = (
                _rope_lane_tables(cos_h, sin_h, jnp.tile(g_q, 2), pair_offs,
                                  HEAD_DIM ** -0.5 * math.log2(math.e)),
                _rope_lane_tables(cos_h, sin_h, jnp.tile(g_k, 2), pair_offs, 1.0),
            )
            q, k, v = _odd_in(x, mod_l, ng[0:1], _odd_weights(p["odd_w_in"][i]), tabs)
            sink = jnp.repeat(p["odd_sink"][i], HEAD_DIM).reshape(1, -1)
            o = _win_attention(q, k, v, sink)
            x = _odd_out(x, mod_l, o, p["odd_w_out"][i].astype(BF16))
            wr = jnp.zeros((D_MODEL, LANES), F32).at[:, :N_EXPERTS].set(p["moe_router"][i])
            x = _moe(x, mod_l, ng[1:2], wr, p["moe_w_up"][i].astype(BF16),
                     p["moe_w_down"][i].astype(BF16))
    return x


def kernel(x_prompt, x_sample, c_prompt, c_sample, ada_w, ada_b, norm_g, even_w_in, even_q_latent_g, even_q_up, even_kv_latent_g, even_kv_up, even_mla_qk_g, even_dil_qk_g, even_w_out, odd_w_in, odd_qk_g, odd_sink, odd_w_out, dense_w_up, dense_w_down, moe_router, moe_w_up, moe_w_down):
    p = dict(ada_w=ada_w, ada_b=ada_b, norm_g=norm_g, even_w_in=even_w_in,
             even_q_latent_g=even_q_latent_g, even_q_up=even_q_up,
             even_kv_latent_g=even_kv_latent_g, even_kv_up=even_kv_up,
             even_mla_qk_g=even_mla_qk_g, even_dil_qk_g=even_dil_qk_g, even_w_out=even_w_out,
             odd_w_in=odd_w_in, odd_qk_g=odd_qk_g, odd_sink=odd_sink, odd_w_out=odd_w_out,
             dense_w_up=dense_w_up, dense_w_down=dense_w_down, moe_router=moe_router,
             moe_w_up=moe_w_up, moe_w_down=moe_w_down)
    return _trunk(x_prompt, c_prompt, p), _trunk(x_sample, c_sample, p)
```

```python
import functools
import math

import jax
import jax.numpy as jnp
from jax import lax
from jax.experimental import pallas as pl
from jax.experimental.pallas import tpu as pltpu
from jax.experimental.pallas import tpu_sc as plsc

D_MODEL = 1024
DEPTH = 4
RMS_EPS = 1e-6
ROPE_THETA = 10000.0
NEG_INF = -1e30
HEAD_DIM = 64
MLA_HEADS = 8
MLA_NOPE = 64
MLA_ROPE = 32
MLA_QK = MLA_NOPE + MLA_ROPE
MLA_V = 64
MLA_Q_RANK = 384
MLA_KV_RANK = 256
DIL_HEADS = 8
DIL_PATTERNS = ((128, 1), (512, 4), (2048, 16))
WIN_Q_HEADS = 16
WIN_KV_HEADS = 4
WIN_HALF = 128
FF_DENSE = 2816
N_EXPERTS = 8
FF_EXPERT = 3584

LANES = 128
KR_OFF = MLA_Q_RANK + MLA_KV_RANK
DIL_OFF = KR_OFF + 2 * LANES
DIL_W = DIL_HEADS * HEAD_DIM

ROW_TILE = 512
MLA_Q_TILE = 256
BAND_Q_TILE = 128
MOE_ROW_TILE = 1024
MOE_FF_CHUNK = 512
MXU_DIM = 256
DENSE_FF_CHUNKS = (6 * MXU_DIM, 5 * MXU_DIM)
assert sum(DENSE_FF_CHUNKS) == FF_DENSE
VMEM_LIMIT = 56 * 1024 * 1024

BF16 = jnp.bfloat16
F32 = jnp.float32


def _cparams(*sem):
    return pltpu.CompilerParams(dimension_semantics=sem, vmem_limit_bytes=VMEM_LIMIT)


def _const_spec(shape):
    nd = len(shape)
    return pl.BlockSpec(shape, lambda *_: (0,) * nd, pipeline_mode=pl.Buffered(1))


def _dot(a, b):
    return jnp.dot(a, b, preferred_element_type=F32)


def _dot_t(a, b):
    return lax.dot_general(a, b, (((1,), (1,)), ((), ())), preferred_element_type=F32)


def _split_bf16(a):
    hi = a.astype(BF16)
    lo = (a - hi.astype(F32)).astype(BF16)
    return hi, lo


def _dot3(a, b):
    ah, al = _split_bf16(a)
    bh, bl = _split_bf16(b)
    return _dot(ah, bh) + (_dot(ah, bl) + _dot(al, bh))


def _sigmoid(x):
    return 1.0 / (1.0 + jnp.exp(-x))


HALF_D = D_MODEL // 2
HI16 = 0xFFFF0000
PLANE_W = HALF_D // 2


def _pack_row(x):
    bits = lambda v: lax.bitcast_convert_type(v.astype(BF16).astype(F32), jnp.uint32)
    return (bits(x[:, :HALF_D]) >> 16) | (bits(x[:, HALF_D:]) & jnp.uint32(HI16))


def _unpack_row(w):
    return (lax.bitcast_convert_type(w << 16, F32),
            lax.bitcast_convert_type(w & jnp.uint32(HI16), F32))


def _norm_mod(x, g, sc, sh):
    ms = jnp.mean(x * x, axis=-1, keepdims=True)
    return (x * lax.rsqrt(ms + RMS_EPS) * g) * (1.0 + sc) + sh


def _rms(x, g):
    ms = jnp.mean(x * x, axis=-1, keepdims=True)
    return x * lax.rsqrt(ms + RMS_EPS) * g


def _lane_lo(shape=(1, LANES)):
    return lax.broadcasted_iota(jnp.int32, shape, len(shape) - 1) < HEAD_DIM


def _rope_slab(slab, partner, tab):
    return slab * tab[0] + partner * tab[1]


def _norm_rope_pair(slab, partner, tab):
    lo = _lane_lo()
    sq = slab * slab
    ss_all = jnp.sum(sq, axis=-1, keepdims=True)
    ss_lo = jnp.sum(jnp.where(lo, sq, 0.0), axis=-1, keepdims=True)
    r_lo = lax.rsqrt(ss_lo * (1.0 / HEAD_DIM) + RMS_EPS)
    r_hi = lax.rsqrt((ss_all - ss_lo) * (1.0 / HEAD_DIM) + RMS_EPS)
    return _rope_slab(slab, partner, tab) * jnp.where(lo, r_lo, r_hi)


def _ada_kernel(c_ref, w_ref, b_ref, o_ref):
    c = c_ref[...]
    o_ref[0] = _dot3(c * _sigmoid(c), w_ref[0]) + b_ref[0]


def _ada_mod(c, ada_w, ada_b):
    bc = c.shape[0]
    nchunk = ada_w.shape[2] // D_MODEL
    out = pl.pallas_call(
        _ada_kernel,
        grid=(DEPTH, nchunk),
        in_specs=[
            pl.BlockSpec((bc, D_MODEL), lambda l, j: (0, 0)),
            pl.BlockSpec((1, D_MODEL, D_MODEL), lambda l, j: (l, 0, j)),
            pl.BlockSpec((1, 1, D_MODEL), lambda l, j: (l, 0, j)),
        ],
        out_specs=pl.BlockSpec((1, bc, D_MODEL), lambda l, j: (l, 0, j)),
        out_shape=jax.ShapeDtypeStruct((DEPTH, bc, nchunk * D_MODEL), F32),
        compiler_params=_cparams("parallel", "parallel"),
        name="ada_mod",
    )(c, ada_w, ada_b.reshape(DEPTH, 1, -1))
    return out.reshape(DEPTH, bc, nchunk, D_MODEL)


def _even_in_kernel(x_ref, mod_ref, ng_ref, w_in_ref, qlg_ref, q_up_ref, kvlg_ref, kv_up_ref,
                    gkn_ref, tq_ref, tk_ref, tqb_ref, tkb_ref,
                    qa_ref, ka_ref, va_ref, qb_ref, kb_ref, vb_ref):
    h = _norm_mod(x_ref[0], ng_ref[...], mod_ref[0, 1:2, :], mod_ref[0, 0:1, :]).astype(BF16)
    proj = _dot(h, w_in_ref[...])

    cq = _rms(proj[:, :MLA_Q_RANK], qlg_ref[...]).astype(BF16)
    qraw = _dot(cq, q_up_ref[...])
    tq = (tq_ref[0], tq_ref[1])
    for hd in range(MLA_HEADS):
        slab = qraw[:, hd * LANES:(hd + 1) * LANES]
        partner = qraw[:, (MLA_HEADS + hd) * LANES:(MLA_HEADS + hd + 1) * LANES]
        ss = jnp.sum(slab * slab, axis=-1, keepdims=True)
        rinv = lax.rsqrt(ss * (1.0 / MLA_QK) + RMS_EPS)
        qa_ref[0, :, hd * LANES:(hd + 1) * LANES] = (
            _rope_slab(slab, partner, tq) * rinv).astype(BF16)

    ckv = _rms(proj[:, MLA_Q_RANK:KR_OFF], kvlg_ref[...]).astype(BF16)
    kvraw = _dot(ckv, kv_up_ref[...])
    kr = proj[:, KR_OFF:KR_OFF + LANES]
    ss_kr = jnp.sum(kr * kr, axis=-1, keepdims=True)
    kr_rot = _rope_slab(kr, proj[:, KR_OFF + LANES:DIL_OFF], (tk_ref[0], tk_ref[1]))
    gkn = gkn_ref[...]
    for hd in range(MLA_HEADS):
        slab = kvraw[:, hd * LANES:(hd + 1) * LANES]
        ss = jnp.sum(slab * slab, axis=-1, keepdims=True) + ss_kr
        rinv = lax.rsqrt(ss * (1.0 / MLA_QK) + RMS_EPS)
        ka_ref[0, :, hd * LANES:(hd + 1) * LANES] = ((slab * gkn + kr_rot) * rinv).astype(BF16)
    va_ref[0] = kvraw[:, MLA_HEADS * LANES:].astype(BF16)

    tqb = (tqb_ref[0], tqb_ref[1])
    tkb = (tkb_ref[0], tkb_ref[1])
    part_off = DIL_OFF + 3 * DIL_W
    for j in range(DIL_W // LANES):
        cq0, ck0 = j * LANES, DIL_W + j * LANES
        qb_ref[0, :, j * LANES:(j + 1) * LANES] = _norm_rope_pair(
            proj[:, DIL_OFF + cq0:DIL_OFF + cq0 + LANES],
            proj[:, part_off + cq0:part_off + cq0 + LANES], tqb).astype(BF16)
        kb_ref[0, :, j * LANES:(j + 1) * LANES] = _norm_rope_pair(
            proj[:, DIL_OFF + ck0:DIL_OFF + ck0 + LANES],
            proj[:, part_off + ck0:part_off + ck0 + LANES], tkb).astype(BF16)
    vb_ref[0] = proj[:, DIL_OFF + 2 * DIL_W:DIL_OFF + 3 * DIL_W].astype(BF16)


def _even_in(x, mod_l, ng, wts, tabs):
    b, t, _ = x.shape
    tm = min(ROW_TILE, t)
    w_in, qlg, q_up, kvlg, kv_up, gkn = wts
    row = lambda c: pl.BlockSpec((1, tm, c), lambda i, j: (i, j, 0))
    tab = pl.BlockSpec((2, tm, LANES), lambda i, j: (0, j, 0))
    hc = DIL_HEADS * HEAD_DIM
    outs = [jax.ShapeDtypeStruct((b, t, c), BF16)
            for c in (MLA_HEADS * LANES, MLA_HEADS * LANES, MLA_HEADS * MLA_V, hc, hc, hc)]
    return pl.pallas_call(
        _even_in_kernel,
        grid=(b, t // tm),
        in_specs=[
            row(D_MODEL),
            pl.BlockSpec((1, 6, D_MODEL), lambda i, j: (i, 0, 0)),
            _const_spec(ng.shape), _const_spec(w_in.shape), _const_spec(qlg.shape),
            _const_spec(q_up.shape), _const_spec(kvlg.shape), _const_spec(kv_up.shape),
            _const_spec(gkn.shape), tab, tab, tab, tab,
        ],
        out_specs=[row(s.shape[2]) for s in outs],
        out_shape=outs,
        compiler_params=_cparams("parallel", "parallel"),
        name="even_in",
    )(x, mod_l, ng, w_in, qlg, q_up, kvlg, kv_up, gkn, *tabs)


def _mla_kernel(q_ref, k_ref, v_ref, o_ref, vaug_s):
    lo = _lane_lo()

    @pl.when(pl.program_id(2) == 0)
    def _():
        v = v_ref[0]
        one = jnp.ones((), BF16)
        vaug_s[0] = jnp.where(lo, v, one)
        vaug_s[1] = jnp.where(lo, one, v)

    outs = []
    for hd in range(2):
        q = q_ref[0, :, hd * LANES:(hd + 1) * LANES]
        k = k_ref[0, :, hd * LANES:(hd + 1) * LANES]
        s = _dot_t(q, k)
        m = jnp.max(s, axis=-1, keepdims=True)
        p = jnp.exp2((s - m).astype(BF16))
        outs.append(_dot(p, vaug_s[hd]))
    num = jnp.where(lo, outs[0], outs[1])
    den = jnp.where(lo, pltpu.roll(outs[0], HEAD_DIM, 1), pltpu.roll(outs[1], HEAD_DIM, 1))
    o_ref[0] = (num * (1.0 / den)).astype(BF16)


def _mla_attention(q, k, v):
    b, t, _ = q.shape
    tq = min(MLA_Q_TILE, t)
    return pl.pallas_call(
        _mla_kernel,
        grid=(b, MLA_HEADS // 2, t // tq),
        in_specs=[
            pl.BlockSpec((1, tq, 2 * LANES), lambda i, c, j: (i, j, c)),
            pl.BlockSpec((1, t, 2 * LANES), lambda i, c, j: (i, 0, c)),
            pl.BlockSpec((1, t, LANES), lambda i, c, j: (i, 0, c)),
        ],
        out_specs=pl.BlockSpec((1, tq, LANES), lambda i, c, j: (i, j, c)),
        out_shape=jax.ShapeDtypeStruct((b, t, MLA_HEADS * MLA_V), BF16),
        scratch_shapes=[pltpu.VMEM((2, t, LANES), BF16)],
        compiler_params=_cparams("parallel", "parallel", "arbitrary"),
        name="mla_attn",
    )(q, k, v)


def _band_chain(qs_list, kw, vw, mask, tq):
    lo = _lane_lo()
    m_lo = jnp.where(lo, 1.0, 0.0).astype(BF16)
    m_hi = jnp.where(lo, 0.0, 1.0).astype(BF16)
    parts = []
    for qs in qs_list:
        parts += [qs * m_lo, qs * m_hi]
    s = mask(_dot_t(jnp.concatenate(parts, axis=0), kw))
    m = jnp.max(s, axis=-1, keepdims=True)
    p = jnp.exp2(s - m)
    l = jnp.sum(p, axis=-1, keepdims=True)
    o = _dot(p.astype(BF16), vw) * (1.0 / l)
    lse = (m + jnp.log2(l)) * math.log(2.0)
    res = []
    for qi in range(len(qs_list)):
        a, b = 2 * qi * tq, (2 * qi + 1) * tq
        res.append((jnp.where(lo, o[a:a + tq], o[b:b + tq]),
                    jnp.where(lo, lse[a:a + tq], lse[b:b + tq])))
    return res


def _band_rel(rows, tq, win, half):
    return (lax.rem(lax.broadcasted_iota(jnp.int32, (rows, win), 0), tq)
            - lax.broadcasted_iota(jnp.int32, (rows, win), 1)) + half


def _band_valid(relh, off, half):
    return lax.bitcast_convert_type(relh + off, jnp.uint32) <= jnp.uint32(2 * half)


def _win_kernel(q_ref, k_ref, v_ref, sink_ref, o_ref, *, seq, tq, win):
    half, nq = WIN_HALF, 2
    relh = _band_rel(2 * nq * tq, tq, win, half)
    bias_mid = jnp.where(_band_valid(relh, half, half), 0.0, NEG_INF)

    def tile(q0, ws, interior):
        if interior:
            mask = lambda s: s + bias_mid
        else:
            valid = _band_valid(relh, q0 - ws, half)
            mask = lambda s: jnp.where(valid, s, NEG_INF)
        qs = [q_ref[0, pl.ds(q0, tq), qi * LANES:(qi + 1) * LANES] for qi in range(nq)]
        res = _band_chain(qs, k_ref[0, pl.ds(ws, win), :], v_ref[0, pl.ds(ws, win), :], mask, tq)
        for qi, (o_slab, lse_slab) in enumerate(res):
            keep = _sigmoid(lse_slab - sink_ref[:, qi * LANES:(qi + 1) * LANES])
            o_ref[0, pl.ds(q0, tq), qi * LANES:(qi + 1) * LANES] = (o_slab * keep).astype(BF16)

    n_t = seq // tq
    tile(0, 0, False)
    if n_t > 1:
        tile(seq - tq, seq - win, False)

    def mid(ti, carry):
        q0 = pl.multiple_of(ti * tq, tq)
        tile(q0, pl.multiple_of(q0 - half, HEAD_DIM), True)
        return carry

    if n_t > 2:
        lax.fori_loop(1, n_t - 1, mid, 0, unroll=2)


def _win_attention(q, k, v, sink):
    b, seq, _ = q.shape
    tq = min(BAND_Q_TILE, seq)
    win = min(tq + 2 * WIN_HALF, seq)
    assert tq >= WIN_HALF or win == seq
    kern = functools.partial(_win_kernel, seq=seq, tq=tq, win=win)
    return pl.pallas_call(
        kern,
        grid=(b, WIN_KV_HEADS),
        in_specs=[
            pl.BlockSpec((1, seq, 2 * LANES), lambda i, c: (i, 0, c)),
            pl.BlockSpec((1, seq, LANES), lambda i, c: (i, 0, c)),
            pl.BlockSpec((1, seq, LANES), lambda i, c: (i, 0, c)),
            pl.BlockSpec((1, 2 * LANES), lambda i, c: (0, c)),
        ],
        out_specs=pl.BlockSpec((1, seq, 2 * LANES), lambda i, c: (i, 0, c)),
        out_shape=jax.ShapeDtypeStruct(q.shape, BF16),
        compiler_params=_cparams("parallel", "parallel"),
        name="win_attn",
    )(q, k, v, sink)


def _dil_kernel(q_ref, k_ref, v_ref, o_ref, qf, kf, vf, qd, kd, vd, acc, lse, of, *, seq):
    qf[...] = q_ref[0].astype(F32)
    kf[...] = k_ref[0].astype(F32)
    vf[...] = v_ref[0].astype(F32)
    for pi in (1, 2):
        d = DIL_PATTERNS[pi][1]
        ln = seq // d
        for r in range(d):
            for src, dst in ((qf, qd), (kf, kd), (vf, vd)):
                dst[pi - 1, r * ln:(r + 1) * ln, :] = src[pl.ds(r, ln, stride=d), :].astype(BF16)

    for pi, (window, d) in enumerate(DIL_PATTERNS):
        ln = seq // d
        half = window // (2 * d)
        tq = min(BAND_Q_TILE, ln)
        win = min(tq + 2 * half, ln)
        n_t = ln // tq
        align = min(tq, HEAD_DIM)
        rel = _band_rel(2 * tq, tq, win, half)
        if pi == 0:
            load = lambda ref, row, n: ref[0, pl.ds(row, n), :]
            srcs = (q_ref, k_ref, v_ref)
        else:
            load = lambda ref, row, n, pi=pi: ref[pi - 1, pl.ds(row, n), :]
            srcs = (qd, kd, vd)

        def body(i, carry, pi=pi, ln=ln, half=half, tq=tq, win=win, n_t=n_t, align=align,
                 rel=rel, load=load, srcs=srcs):
            r = i // n_t
            q0 = (i - r * n_t) * tq
            ws = 0 if win == ln else jnp.clip(q0 - half, 0, ln - win)
            valid = _band_valid(rel, q0 - ws, half)
            rq = pl.multiple_of(r * ln + q0, align)
            rk = pl.multiple_of(r * ln + ws, align)
            ((o_slab, lse_slab),) = _band_chain(
                [load(srcs[0], rq, tq)], load(srcs[1], rk, win), load(srcs[2], rk, win),
                lambda s: jnp.where(valid, s, NEG_INF), tq)
            acc[pi, pl.ds(rq, tq), :] = o_slab
            lse[pi, pl.ds(rq, tq), :] = lse_slab
            return carry

        lax.fori_loop(0, seq // tq, body, 0, unroll=min(8, seq // tq))

    d2, d3 = DIL_PATTERNS[1][1], DIL_PATTERNS[2][1]
    l2n, l3n = seq // d2, seq // d3
    for r in range(d3):
        s1 = pl.ds(r, l3n, stride=d3)
        s2 = pl.ds((r % d2) * l2n + r // d2, l3n, stride=d3 // d2)
        s3 = pl.ds(r * l3n, l3n)
        l1, l2, l3 = lse[0, s1, :], lse[1, s2, :], lse[2, s3, :]
        m = jnp.maximum(jnp.maximum(l1, l2), l3)
        e1, e2, e3 = jnp.exp(l1 - m), jnp.exp(l2 - m), jnp.exp(l3 - m)
        of[s1, :] = ((e1 * acc[0, s1, :] + e2 * acc[1, s2, :] + e3 * acc[2, s3, :])
                     * (1.0 / (e1 + e2 + e3)))
    o_ref[0] = of[...].astype(BF16)


def _dil_attention(q, k, v):
    b, seq, c = q.shape
    blk = pl.BlockSpec((1, seq, LANES), lambda i, j: (i, 0, j))
    f32s = pltpu.VMEM((seq, LANES), F32)
    res = pltpu.VMEM((2, seq, LANES), BF16)
    pat = pltpu.VMEM((len(DIL_PATTERNS), seq, LANES), F32)
    return pl.pallas_call(
        functools.partial(_dil_kernel, seq=seq),
        grid=(b, c // LANES),
        in_specs=[blk, blk, blk],
        out_specs=blk,
        out_shape=jax.ShapeDtypeStruct(q.shape, BF16),
        scratch_shapes=[f32s, f32s, f32s, res, res, res, pat, pat, f32s],
        compiler_params=_cparams("parallel", "parallel"),
        name="dil_attn",
    )(q, k, v)


def _even_out_kernel(x_ref, mod_ref, oa_ref, ob_ref, w_ref, y_ref):
    half = w_ref.shape[0] // 2
    mix = _dot(oa_ref[0], w_ref[:half, :]) + _dot(ob_ref[0], w_ref[half:, :])
    y_ref[0] = x_ref[0] + mod_ref[0, 2:3, :] * mix


def _even_out(x, mod_l, oa, ob, w_out):
    b, t, _ = x.shape
    tm = min(ROW_TILE, t)
    row = lambda c: pl.BlockSpec((1, tm, c), lambda i, j: (i, j, 0))
    return pl.pallas_call(
        _even_out_kernel,
        grid=(b, t // tm),
        in_specs=[row(D_MODEL), pl.BlockSpec((1, 6, D_MODEL), lambda i, j: (i, 0, 0)),
                  row(oa.shape[2]), row(ob.shape[2]), _const_spec(w_out.shape)],
        out_specs=row(D_MODEL),
        out_shape=jax.ShapeDtypeStruct(x.shape, F32),
        compiler_params=_cparams("parallel", "parallel"),
        name="even_out",
    )(x, mod_l, oa, ob, w_out)


def _dense_ffn_kernel(x_ref, mod_ref, ng_ref, wu_ref, wd_ref, y_ref):
    x = x_ref[0]
    h = _norm_mod(x, ng_ref[...], mod_ref[0, 4:5, :], mod_ref[0, 3:4, :]).astype(BF16)
    ff = wd_ref.shape[0]
    acc = None
    c0 = 0
    for cw in DENSE_FF_CHUNKS:
        g = _dot(h, wu_ref[:, c0:c0 + cw])
        u = _dot(h, wu_ref[:, ff + c0:ff + c0 + cw])
        a = (g * _sigmoid(g) * u).astype(BF16)
        d = _dot(a, wd_ref[c0:c0 + cw, :])
        acc = d if acc is None else acc + d
        c0 += cw
    y_ref[0] = x + mod_ref[0, 5:6, :] * acc


def _dense_ffn(x, mod_l, ng, w_up, w_down):
    b, t, _ = x.shape
    tm = min(ROW_TILE, t)
    row = pl.BlockSpec((1, tm, D_MODEL), lambda i, j: (i, j, 0))
    return pl.pallas_call(
        _dense_ffn_kernel,
        grid=(b, t // tm),
        in_specs=[row, pl.BlockSpec((1, 6, D_MODEL), lambda i, j: (i, 0, 0)),
                  _const_spec(ng.shape), _const_spec(w_up.shape), _const_spec(w_down.shape)],
        out_specs=row,
        out_shape=jax.ShapeDtypeStruct(x.shape, F32),
        compiler_params=_cparams("parallel", "parallel"),
        name="dense_ffn",
    )(x, mod_l, ng, w_up, w_down)


def _odd_in_kernel(x_ref, mod_ref, ng_ref, w_in_ref, tq_ref, tk_ref, q_ref, k_ref, v_ref):
    h = _norm_mod(x_ref[0], ng_ref[...], mod_ref[0, 1:2, :], mod_ref[0, 0:1, :]).astype(BF16)
    proj = _dot(h, w_in_ref[...])
    tq = (tq_ref[0], tq_ref[1])
    tk = (tk_ref[0], tk_ref[1])
    nq = WIN_Q_HEADS * HEAD_DIM // LANES
    part = (nq + 2 * WIN_KV_HEADS) * LANES
    for j in range(nq):
        q_ref[0, :, j * LANES:(j + 1) * LANES] = _norm_rope_pair(
            proj[:, j * LANES:(j + 1) * LANES],
            proj[:, part + j * LANES:part + (j + 1) * LANES], tq).astype(BF16)
    for j in range(WIN_KV_HEADS):
        c0 = (nq + j) * LANES
        k_ref[0, :, j * LANES:(j + 1) * LANES] = _norm_rope_pair(
            proj[:, c0:c0 + LANES], proj[:, part + c0:part + c0 + LANES], tk).astype(BF16)
    v_ref[0] = proj[:, (nq + WIN_KV_HEADS) * LANES:part].astype(BF16)


def _odd_in(x, mod_l, ng, w_in, tabs):
    b, t, _ = x.shape
    tm = min(ROW_TILE, t)
    row = lambda c: pl.BlockSpec((1, tm, c), lambda i, j: (i, j, 0))
    tab = pl.BlockSpec((2, tm, LANES), lambda i, j: (0, j, 0))
    widths = (WIN_Q_HEADS * HEAD_DIM, WIN_KV_HEADS * LANES, WIN_KV_HEADS * LANES)
    return pl.pallas_call(
        _odd_in_kernel,
        grid=(b, t // tm),
        in_specs=[row(D_MODEL), pl.BlockSpec((1, 6, D_MODEL), lambda i, j: (i, 0, 0)),
                  _const_spec(ng.shape), _const_spec(w_in.shape), tab, tab],
        out_specs=[row(c) for c in widths],
        out_shape=[jax.ShapeDtypeStruct((b, t, c), BF16) for c in widths],
        compiler_params=_cparams("parallel", "parallel"),
        name="odd_in",
    )(x, mod_l, ng, w_in, *tabs)


def _odd_out_kernel(x_ref, mod_ref, o_ref, w_ref, y_ref):
    y_ref[0] = x_ref[0] + mod_ref[0, 2:3, :] * _dot(o_ref[0], w_ref[...])


def _odd_out(x, mod_l, o, w_out):
    b, t, _ = x.shape
    tm = min(ROW_TILE, t)
    row = pl.BlockSpec((1, tm, D_MODEL), lambda i, j: (i, j, 0))
    return pl.pallas_call(
        _odd_out_kernel,
        grid=(b, t // tm),
        in_specs=[row, pl.BlockSpec((1, 6, D_MODEL), lambda i, j: (i, 0, 0)), row,
                  _const_spec(w_out.shape)],
        out_specs=row,
        out_shape=jax.ShapeDtypeStruct(x.shape, F32),
        compiler_params=_cparams("parallel", "parallel"),
        name="odd_out",
    )(x, mod_l, o, w_out)


def _router_kernel(x_ref, mod_ref, ng_ref, wr_ref, h_ref, gate_ref, sel_ref):
    h = _norm_mod(x_ref[0], ng_ref[...], mod_ref[0, 4:5, :], mod_ref[0, 3:4, :])
    w = _pack_row(h)
    h_ref[0, 0] = w[:, :PLANE_W]
    h_ref[1, 0] = w[:, PLANE_W:]
    lane = lax.broadcasted_iota(jnp.int32, (1, LANES), 1).astype(F32)
    logits = jnp.where(lane < N_EXPERTS, _dot3(h, wr_ref[...]), NEG_INF)
    e = jnp.exp(logits - jnp.max(logits, axis=-1, keepdims=True))
    p = e * (1.0 / jnp.sum(e, axis=-1, keepdims=True))
    p = jnp.where(lane < N_EXPERTS, p, -1.0)
    m1 = jnp.max(p, axis=-1, keepdims=True)
    i1 = jnp.min(jnp.where(p == m1, lane, float(LANES)), axis=-1, keepdims=True)
    s1 = lane == i1
    p2 = jnp.where(s1, -1.0, p)
    m2 = jnp.max(p2, axis=-1, keepdims=True)
    i2 = jnp.min(jnp.where(p2 == m2, lane, float(LANES)), axis=-1, keepdims=True)
    s2 = lane == i2
    inv = 1.0 / (m1 + m2)
    gate_ref[0] = jnp.where(s1, m1 * inv, 0.0) + jnp.where(s2, m2 * inv, 0.0)
    sel_ref[0] = jnp.where(s1, 1.0, 0.0) + jnp.where(s2, 1.0, 0.0)


def _router(x, mod_l, ng, wr):
    b, t, _ = x.shape
    tm = min(ROW_TILE, t)
    row = lambda c: pl.BlockSpec((1, tm, c), lambda i, j: (i, j, 0))
    return pl.pallas_call(
        _router_kernel,
        grid=(b, t // tm),
        in_specs=[row(D_MODEL), pl.BlockSpec((1, 6, D_MODEL), lambda i, j: (i, 0, 0)),
                  _const_spec(ng.shape), _const_spec(wr.shape)],
        out_specs=[pl.BlockSpec((2, 1, tm, PLANE_W), lambda i, j: (0, i, j, 0)), row(LANES), row(LANES)],
        out_shape=[jax.ShapeDtypeStruct((2, b, t, PLANE_W), jnp.uint32),
                   jax.ShapeDtypeStruct((b, t, LANES), F32),
                   jax.ShapeDtypeStruct((b, t, LANES), F32)],
        compiler_params=_cparams("parallel", "parallel"),
        name="moe_router",
    )(x, mod_l, ng, wr)


SC_WINDOW = 128
SC_COLS = PLANE_W


def _sc_mesh():
    return plsc.VectorSubcoreMesh(core_axis_name="core", subcore_axis_name="subcore")


def _scatter_rows_sc(src, pos_a, pos_b, n_out):
    n = src.shape[0]

    @pl.kernel(out_type=jax.ShapeDtypeStruct((n_out, SC_COLS), src.dtype), mesh=_sc_mesh(),
               scratch_types=[])
    def scatter(src_hbm, ia_hbm, ib_hbm, dst_hbm):
        def body(rows_vmem, ia_vmem, ib_vmem):
            pltpu.sync_copy(rows_vmem, dst_hbm.at[ia_vmem.at[0]])
            pltpu.sync_copy(rows_vmem, dst_hbm.at[ib_vmem.at[0]])

        idx_spec = pl.BlockSpec((1, SC_WINDOW), lambda i: (0, i))
        pltpu.emit_pipeline(
            body,
            grid=(n // SC_WINDOW,),
            in_specs=[pl.BlockSpec((SC_WINDOW, SC_COLS), lambda i: (i, 0)), idx_spec, idx_spec],
            out_specs=[],
            core_axis_name=("core", "subcore"),
            dimension_semantics=(pltpu.PARALLEL,),
        )(src_hbm, ia_hbm, ib_hbm)

    return scatter(src, pos_a.reshape(1, n), pos_b.reshape(1, n))


def _gather_rows_sc(src, pos):
    n = pos.shape[0]

    @pl.kernel(out_type=jax.ShapeDtypeStruct((n, SC_COLS), src.dtype), mesh=_sc_mesh(),
               scratch_types=[])
    def gather(src_hbm, i_hbm, dst_hbm):
        def body(i_vmem, rows_vmem):
            pltpu.sync_copy(src_hbm.at[i_vmem.at[0]], rows_vmem)

        pltpu.emit_pipeline(
            body,
            grid=(n // SC_WINDOW,),
            in_specs=[pl.BlockSpec((1, SC_WINDOW), lambda i: (0, i))],
            out_specs=[pl.BlockSpec((SC_WINDOW, SC_COLS), lambda i: (i, 0))],
            core_axis_name=("core", "subcore"),
            dimension_semantics=(pltpu.PARALLEL,),
        )(i_hbm, dst_hbm)

    return gather(src, pos.reshape(1, n))


def _expert_kernel(te_ref, nu_ref, x0_ref, x1_ref, wg_ref, wu_ref, wd_ref, ys_ref, xb_ref, acc_ref):
    i, j = pl.program_id(0), pl.program_id(1)
    used = i < nu_ref[0]

    @pl.when(jnp.logical_and(used, j == 0))
    def _():
        for pi, x_ref in enumerate((x0_ref, x1_ref)):
            lo, hi = _unpack_row(x_ref[0])
            xb_ref[:, pi * PLANE_W:(pi + 1) * PLANE_W] = lo.astype(BF16)
            xb_ref[:, HALF_D + pi * PLANE_W:HALF_D + (pi + 1) * PLANE_W] = hi.astype(BF16)
        acc_ref[...] = jnp.zeros_like(acc_ref)

    @pl.when(used)
    def _():
        xb = xb_ref[...]
        g = _dot(xb, wg_ref[0])
        u = _dot(xb, wu_ref[0])
        a = (g * _sigmoid(g) * u).astype(BF16)
        acc_ref[...] += _dot(a, wd_ref[0])

    last = j == pl.num_programs(1) - 1

    @pl.when(jnp.logical_and(used, last))
    def _():
        w = _pack_row(acc_ref[...])
        ys_ref[0] = w[:, :PLANE_W]
        ys_ref[1] = w[:, PLANE_W:]

    @pl.when(jnp.logical_and(jnp.logical_not(used), last))
    def _():
        ys_ref[...] = jnp.zeros_like(ys_ref)


def _experts(xs, tile_expert, n_used, w_up, w_down):
    p = xs.shape[1]
    nck = FF_EXPERT // MOE_FF_CHUNK
    tm = MOE_ROW_TILE

    def chunk(i, j, te, nu):
        return jnp.where(i < nu[0], j, nck - 1)

    grid_spec = pltpu.PrefetchScalarGridSpec(
        num_scalar_prefetch=2,
        grid=(p // tm, nck),
        in_specs=[
            pl.BlockSpec((1, tm, PLANE_W), lambda i, j, te, nu: (0, i, 0)),
            pl.BlockSpec((1, tm, PLANE_W), lambda i, j, te, nu: (1, i, 0)),
            pl.BlockSpec((1, D_MODEL, MOE_FF_CHUNK), lambda i, j, te, nu: (te[i], 0, chunk(i, j, te, nu))),
            pl.BlockSpec((1, D_MODEL, MOE_FF_CHUNK),
                         lambda i, j, te, nu: (te[i], 0, nck + chunk(i, j, te, nu))),
            pl.BlockSpec((1, MOE_FF_CHUNK, D_MODEL), lambda i, j, te, nu: (te[i], chunk(i, j, te, nu), 0)),
        ],
        out_specs=pl.BlockSpec((2, tm, PLANE_W), lambda i, j, te, nu: (0, i, 0)),
        scratch_shapes=[pltpu.VMEM((tm, D_MODEL), BF16), pltpu.VMEM((tm, D_MODEL), F32)],
    )
    return pl.pallas_call(
        _expert_kernel,
        grid_spec=grid_spec,
        out_shape=jax.ShapeDtypeStruct((2, p, PLANE_W), jnp.uint32),
        compiler_params=_cparams("arbitrary", "arbitrary"),
        name="moe_experts",
    )(tile_expert, n_used, xs, xs, w_up, w_up, w_down)


def _combine_kernel(x_ref, mod_ref, g_ref, a0_ref, b0_ref, a1_ref, b1_ref, o_ref):
    g = g_ref[0]
    ga, gb = g[:, 0:1], g[:, 1:2]
    for pi, (a_ref, b_ref) in enumerate(((a0_ref, b0_ref), (a1_ref, b1_ref))):
        a_lo, a_hi = _unpack_row(a_ref[0, 0, 0])
        b_lo, b_hi = _unpack_row(b_ref[0, 0, 0])
        for c0, ff in ((pi * PLANE_W, ga * a_lo + gb * b_lo),
                       (HALF_D + pi * PLANE_W, ga * a_hi + gb * b_hi)):
            cols = slice(c0, c0 + PLANE_W)
            o_ref[0, :, cols] = x_ref[0, :, cols] + mod_ref[0, 5:6, cols] * ff


def _combine(x, mod_l, gate_ab, yg):
    b, t, _ = x.shape
    tm = min(ROW_TILE, t)
    row = pl.BlockSpec((1, tm, D_MODEL), lambda i, j: (i, j, 0))
    part = lambda pi, ab: pl.BlockSpec((1, 1, 1, tm, PLANE_W), lambda i, j: (pi, ab, i, j, 0))
    return pl.pallas_call(
        _combine_kernel,
        grid=(b, t // tm),
        in_specs=[row, pl.BlockSpec((1, 6, D_MODEL), lambda i, j: (i, 0, 0)),
                  pl.BlockSpec((1, tm, 2), lambda i, j: (i, j, 0)),
                  part(0, 0), part(0, 1), part(1, 0), part(1, 1)],
        out_specs=row,
        out_shape=jax.ShapeDtypeStruct(x.shape, F32),
        compiler_params=_cparams("parallel", "parallel"),
        name="moe_combine",
    )(x, mod_l, gate_ab, yg, yg, yg, yg)


def _moe(x, mod_l, ng, wr, w_up, w_down):
    b, t, _ = x.shape
    n = b * t
    h, gates, sel = _router(x, mod_l, ng, wr)
    gates = gates.reshape(n, LANES)[:, :N_EXPERTS]
    sel = sel.reshape(n, LANES)[:, :N_EXPERTS] > 0.5

    tm = MOE_ROW_TILE
    n_tiles = 2 * n // tm + N_EXPERTS
    p = n_tiles * tm
    sel_i = sel.astype(jnp.int32)
    rank = jnp.cumsum(sel_i, axis=0) - sel_i
    counts = jnp.sum(sel_i, axis=0)
    padded = (counts + tm - 1) // tm * tm
    ends = jnp.cumsum(padded)
    pos = (ends - padded)[None, :] + rank
    tile_start = jnp.arange(n_tiles, dtype=jnp.int32) * tm
    tile_expert = jnp.minimum(jnp.sum((tile_start[:, None] >= ends[None, :]).astype(jnp.int32), axis=1),
                              N_EXPERTS - 1)
    n_used = (ends[-1] // tm).astype(jnp.int32).reshape(1)
    order = jnp.cumsum(sel_i, axis=1)
    first, second = sel & (order == 1), sel & (order == 2)
    pos_a = jnp.sum(jnp.where(first, pos, 0), axis=1)
    pos_b = jnp.sum(jnp.where(second, pos, 0), axis=1)
    gate_ab = jnp.stack([jnp.sum(jnp.where(first, gates, 0.0), axis=1),
                         jnp.sum(jnp.where(second, gates, 0.0), axis=1)], axis=1)

    xs = _scatter_rows_sc(h.reshape(2 * n, PLANE_W), jnp.concatenate([pos_a, pos_a + p]),
                          jnp.concatenate([pos_b, pos_b + p]), 2 * p)
    ys = _experts(xs.reshape(2, p, PLANE_W), tile_expert, n_used, w_up, w_down)
    yg = _gather_rows_sc(ys.reshape(2 * p, PLANE_W),
                         jnp.concatenate([pos_a, pos_b, pos_a + p, pos_b + p]))
    return _combine(x, mod_l, gate_ab.reshape(b, t, 2), yg.reshape(2, 2, b, t, PLANE_W))


def _rope_tables(t, dim):
    inv = ROPE_THETA ** (-jnp.arange(0, dim, 2, dtype=F32) / dim)
    ang = jnp.arange(t, dtype=F32)[:, None] * inv[None, :]
    return jnp.cos(ang), jnp.sin(ang)


def _rope_lane_tables(cos, sin, gain, offs, scale):
    t, hw = cos.shape
    c, s = jnp.ones((t, LANES), F32), jnp.zeros((t, LANES), F32)
    g_up, g_dn = jnp.roll(gain, hw), jnp.roll(gain, -hw)
    g_partner = jnp.zeros((LANES,), F32)
    for o in offs:
        c = c.at[:, o:o + hw].set(cos).at[:, o + hw:o + 2 * hw].set(cos)
        s = s.at[:, o:o + hw].set(-sin).at[:, o + hw:o + 2 * hw].set(sin)
        g_partner = g_partner.at[o:o + hw].set(g_dn[o:o + hw]).at[o + hw:o + 2 * hw].set(
            g_up[o + hw:o + 2 * hw])
    return jnp.stack([c * gain, s * g_partner]) * scale


def _swap_halves(w, width):
    d, c = w.shape
    return w.reshape(d, c // width, 2, width // 2)[:, :, ::-1, :].reshape(d, c)


def _pad_lanes(v, width=LANES):
    return jnp.zeros((width,), F32).at[:v.shape[0]].set(v)


def _even_weights(w_in, qlg, q_up, kvlg, kv_up):
    w_kr = w_in[:, KR_OFF:KR_OFF + MLA_ROPE]
    slab = lambda w: jnp.zeros((D_MODEL, LANES), F32).at[:, MLA_NOPE:MLA_QK].set(w)
    w_dil = w_in[:, KR_OFF + MLA_ROPE:]
    n_qk = 2 * DIL_HEADS * HEAD_DIM
    w_in_p = jnp.concatenate([w_in[:, :KR_OFF], slab(w_kr), slab(_swap_halves(w_kr, MLA_ROPE)),
                              w_dil, _swap_halves(w_dil[:, :n_qk], HEAD_DIM)], axis=1)
    q3 = q_up.reshape(MLA_Q_RANK, MLA_HEADS, MLA_QK)
    q_rope_p = _swap_halves(q3[:, :, MLA_NOPE:].reshape(MLA_Q_RANK, -1), MLA_ROPE)
    q_part = jnp.pad(q_rope_p.reshape(MLA_Q_RANK, MLA_HEADS, MLA_ROPE),
                     ((0, 0), (0, 0), (MLA_NOPE, LANES - MLA_QK)))
    q_up_p = jnp.concatenate([jnp.pad(q3, ((0, 0), (0, 0), (0, LANES - MLA_QK))), q_part],
                             axis=1).reshape(MLA_Q_RANK, 2 * MLA_HEADS * LANES)
    kv3 = kv_up.reshape(MLA_KV_RANK, MLA_HEADS, MLA_NOPE + MLA_V)
    k_p = jnp.pad(kv3[:, :, :MLA_NOPE], ((0, 0), (0, 0), (0, LANES - MLA_NOPE)))
    kv_up_p = jnp.concatenate([k_p.reshape(MLA_KV_RANK, -1),
                               kv3[:, :, MLA_NOPE:].reshape(MLA_KV_RANK, -1)], axis=1)
    return (w_in_p.astype(BF16), qlg.reshape(1, -1), q_up_p.astype(BF16), kvlg.reshape(1, -1),
            kv_up_p.astype(BF16))


def _odd_weights(w_in):
    nq = WIN_Q_HEADS * HEAD_DIM
    nkv = WIN_KV_HEADS * HEAD_DIM
    dup = lambda w: jnp.concatenate([w.reshape(D_MODEL, WIN_KV_HEADS, 1, HEAD_DIM)] * 2,
                                    axis=2).reshape(D_MODEL, 2 * nkv)
    w_q, w_k = w_in[:, :nq], dup(w_in[:, nq:nq + nkv])
    return jnp.concatenate([w_q, w_k, dup(w_in[:, nq + nkv:]), _swap_halves(w_q, HEAD_DIM),
                            _swap_halves(w_k, HEAD_DIM)], axis=1).astype(BF16)


def _trunk(x, c, p):
    b, t, _ = x.shape
    mod = _ada_mod(c, p["ada_w"], p["ada_b"])
    cos_r, sin_r = _rope_tables(t, MLA_ROPE)
    cos_h, sin_h = _rope_tables(t, HEAD_DIM)
    pair_offs = (0, HEAD_DIM)
    for l in range(DEPTH):
        i = l // 2
        mod_l = mod[l]
        ng = p["norm_g"][l]
        if l % 2 == 0:
            g_q, g_k = p["even_mla_qk_g"][i, 0], p["even_mla_qk_g"][i, 1]
            g_qb, g_kb = p["even_dil_qk_g"][i, 0], p["even_dil_qk_g"][i, 1]
            tabs = (
                _rope_lane_tables(cos_r, sin_r, _pad_lanes(g_q), (MLA_NOPE,),
                                  MLA_QK ** -0.5 * math.log2(math.e)),
                _rope_lane_tables(cos_r, sin_r, _pad_lanes(g_k).at[:MLA_NOPE].set(0.0), (MLA_NOPE,), 1.0),
                _rope_lane_tables(cos_h, sin_h, jnp.tile(g_qb, 2), pair_offs,
                                  HEAD_DIM ** -0.5 * math.log2(math.e)),
                _rope_lane_tables(cos_h, sin_h, jnp.tile(g_kb, 2), pair_offs, 1.0),
            )
            gkn = _pad_lanes(g_k[:MLA_NOPE]).reshape(1, LANES)
            wts = _even_weights(p["even_w_in"][i], p["even_q_latent_g"][i], p["even_q_up"][i],
                                p["even_kv_latent_g"][i], p["even_kv_up"][i]) + (gkn,)
            qa, ka, va, qb, kb, vb = _even_in(x, mod_l, ng[0:1], wts, tabs)
            oa = _mla_attention(qa, ka, va)
            ob = _dil_attention(qb, kb, vb)
            x = _even_out(x, mod_l, oa, ob, p["even_w_out"][i].astype(BF16))
            x = _dense_ffn(x, mod_l, ng[1:2], p["dense_w_up"][i].astype(BF16),
                           p["dense_w_down"][i].astype(BF16))
        else:
            g_q, g_k = p["odd_qk_g"][i, 0], p["odd_qk_g"][i, 1]
            tabs = (
                _rope_lane_tables(cos_h, sin_h, jnp.tile(g_q, 2), pair_offs,
                                  HEAD_DIM ** -0.5 * math.log2(math.e)),
                _rope_lane_tables(cos_h, sin_h, jnp.tile(g_k, 2), pair_offs, 1.0),
            )
            q, k, v = _odd_in(x, mod_l, ng[0:1], _odd_weights(p["odd_w_in"][i]), tabs)
            sink = jnp.repeat(p["odd_sink"][i], HEAD_DIM).reshape(1, -1)
            o = _win_attention(q, k, v, sink)
            x = _odd_out(x, mod_l, o, p["odd_w_out"][i].astype(BF16))
            wr = jnp.zeros((D_MODEL, LANES), F32).at[:, :N_EXPERTS].set(p["moe_router"][i])
            x = _moe(x, mod_l, ng[1:2], wr, p["moe_w_up"][i].astype(BF16),
                     p["moe_w_down"][i].astype(BF16))
    return x


def kernel(x_prompt, x_sample, c_prompt, c_sample, ada_w, ada_b, norm_g, even_w_in, even_q_latent_g, even_q_up, even_kv_latent_g, even_kv_up, even_mla_qk_g, even_dil_qk_g, even_w_out, odd_w_in, odd_qk_g, odd_sink, odd_w_out, dense_w_up, dense_w_down, moe_router, moe_w_up, moe_w_down):
    p = dict(ada_w=ada_w, ada_b=ada_b, norm_g=norm_g, even_w_in=even_w_in,
             even_q_latent_g=even_q_latent_g, even_q_up=even_q_up,
             even_kv_latent_g=even_kv_latent_g, even_kv_up=even_kv_up,
             even_mla_qk_g=even_mla_qk_g, even_dil_qk_g=even_dil_qk_g, even_w_out=even_w_out,
             odd_w_in=odd_w_in, odd_qk_g=odd_qk_g, odd_sink=odd_sink, odd_w_out=odd_w_out,
             dense_w_up=dense_w_up, dense_w_down=dense_w_down, moe_router=moe_router,
             moe_w_up=moe_w_up, moe_w_down=moe_w_down)
    return _trunk(x_prompt, c_prompt, p), _trunk(x_sample, c_sample, p)
```

```python
import functools
import math

import jax
import jax.numpy as jnp
from jax import lax
from jax.experimental import pallas as pl
from jax.experimental.pallas import tpu as pltpu
from jax.experimental.pallas import tpu_sc as plsc

D_MODEL = 1024
DEPTH = 4
RMS_EPS = 1e-6
ROPE_THETA = 10000.0
NEG_INF = -1e30
HEAD_DIM = 64
MLA_HEADS = 8
MLA_NOPE = 64
MLA_ROPE = 32
MLA_QK = MLA_NOPE + MLA_ROPE
MLA_V = 64
MLA_Q_RANK = 384
MLA_KV_RANK = 256
DIL_HEADS = 8
DIL_PATTERNS = ((128, 1), (512, 4), (2048, 16))
WIN_Q_HEADS = 16
WIN_KV_HEADS = 4
WIN_HALF = 128
FF_DENSE = 2816
N_EXPERTS = 8
FF_EXPERT = 3584

LANES = 128
KR_OFF = MLA_Q_RANK + MLA_KV_RANK
DIL_OFF = KR_OFF + 2 * LANES
DIL_W = DIL_HEADS * HEAD_DIM

ROW_TILE = 512
MLA_Q_TILE = 512
MLA_SUB_TILES = 2
BAND_Q_TILE = 128
BAND_UNROLL = 4
MOE_ROW_TILE = 1024
MOE_FF_CHUNK = 512
MXU_DIM = 256
DENSE_FF_CHUNKS = (6 * MXU_DIM, 5 * MXU_DIM)
assert sum(DENSE_FF_CHUNKS) == FF_DENSE
VMEM_LIMIT = 56 * 1024 * 1024

BF16 = jnp.bfloat16
F32 = jnp.float32


def _cparams(*sem):
    return pltpu.CompilerParams(dimension_semantics=sem, vmem_limit_bytes=VMEM_LIMIT)


def _const_spec(shape):
    nd = len(shape)
    return pl.BlockSpec(shape, lambda *_: (0,) * nd, pipeline_mode=pl.Buffered(1))


def _dot(a, b):
    return jnp.dot(a, b, preferred_element_type=F32)


def _dot_t(a, b):
    return lax.dot_general(a, b, (((1,), (1,)), ((), ())), preferred_element_type=F32)


def _split_bf16(a):
    hi = a.astype(BF16)
    lo = (a - hi.astype(F32)).astype(BF16)
    return hi, lo


def _dot3(a, b):
    ah, al = _split_bf16(a)
    bh, bl = _split_bf16(b)
    return _dot(ah, bh) + (_dot(ah, bl) + _dot(al, bh))


def _sigmoid(x):
    return 1.0 / (1.0 + jnp.exp(-x))


HALF_D = D_MODEL // 2
HI16 = 0xFFFF0000
PLANE_W = HALF_D // 2


def _pack_row(x):
    bits = lambda v: lax.bitcast_convert_type(v.astype(BF16).astype(F32), jnp.uint32)
    return (bits(x[:, :HALF_D]) >> 16) | (bits(x[:, HALF_D:]) & jnp.uint32(HI16))


def _unpack_row(w):
    return (lax.bitcast_convert_type(w << 16, F32),
            lax.bitcast_convert_type(w & jnp.uint32(HI16), F32))


def _norm_mod(x, g, sc, sh):
    ms = jnp.mean(x * x, axis=-1, keepdims=True)
    return (x * lax.rsqrt(ms + RMS_EPS) * g) * (1.0 + sc) + sh


def _rms(x, g):
    ms = jnp.mean(x * x, axis=-1, keepdims=True)
    return x * lax.rsqrt(ms + RMS_EPS) * g


def _lane_lo(shape=(1, LANES)):
    return lax.broadcasted_iota(jnp.int32, shape, len(shape) - 1) < HEAD_DIM


def _rope_slab(slab, partner, tab):
    return slab * tab[0] + partner * tab[1]


def _norm_rope_pair(slab, partner, tab):
    lo = _lane_lo()
    sq = slab * slab
    ss_all = jnp.sum(sq, axis=-1, keepdims=True)
    ss_lo = jnp.sum(jnp.where(lo, sq, 0.0), axis=-1, keepdims=True)
    r_lo = lax.rsqrt(ss_lo * (1.0 / HEAD_DIM) + RMS_EPS)
    r_hi = lax.rsqrt((ss_all - ss_lo) * (1.0 / HEAD_DIM) + RMS_EPS)
    return _rope_slab(slab, partner, tab) * jnp.where(lo, r_lo, r_hi)


def _ada_kernel(c_ref, w_ref, b_ref, o_ref):
    c = c_ref[...]
    o_ref[0] = _dot3(c * _sigmoid(c), w_ref[0]) + b_ref[0]


def _ada_mod(c, ada_w, ada_b):
    bc = c.shape[0]
    nchunk = ada_w.shape[2] // D_MODEL
    out = pl.pallas_call(
        _ada_kernel,
        grid=(DEPTH, nchunk),
        in_specs=[
            pl.BlockSpec((bc, D_MODEL), lambda l, j: (0, 0)),
            pl.BlockSpec((1, D_MODEL, D_MODEL), lambda l, j: (l, 0, j)),
            pl.BlockSpec((1, 1, D_MODEL), lambda l, j: (l, 0, j)),
        ],
        out_specs=pl.BlockSpec((1, bc, D_MODEL), lambda l, j: (l, 0, j)),
        out_shape=jax.ShapeDtypeStruct((DEPTH, bc, nchunk * D_MODEL), F32),
        compiler_params=_cparams("parallel", "parallel"),
        name="ada_mod",
    )(c, ada_w, ada_b.reshape(DEPTH, 1, -1))
    return out.reshape(DEPTH, bc, nchunk, D_MODEL)


def _even_in_kernel(x_ref, mod_ref, ng_ref, w_in_ref, qlg_ref, q_up_ref, kvlg_ref, kv_up_ref,
                    gkn_ref, tq_ref, tk_ref, tqb_ref, tkb_ref,
                    qa_ref, ka_ref, va_ref, qb_ref, kb_ref, vb_ref):
    h = _norm_mod(x_ref[0], ng_ref[...], mod_ref[0, 1:2, :], mod_ref[0, 0:1, :]).astype(BF16)
    proj = _dot(h, w_in_ref[...])

    cq = _rms(proj[:, :MLA_Q_RANK], qlg_ref[...]).astype(BF16)
    qraw = _dot(cq, q_up_ref[...])
    tq = (tq_ref[0], tq_ref[1])
    for hd in range(MLA_HEADS):
        slab = qraw[:, hd * LANES:(hd + 1) * LANES]
        partner = qraw[:, (MLA_HEADS + hd) * LANES:(MLA_HEADS + hd + 1) * LANES]
        ss = jnp.sum(slab * slab, axis=-1, keepdims=True)
        rinv = lax.rsqrt(ss * (1.0 / MLA_QK) + RMS_EPS)
        qa_ref[0, :, hd * LANES:(hd + 1) * LANES] = (
            _rope_slab(slab, partner, tq) * rinv).astype(BF16)

    ckv = _rms(proj[:, MLA_Q_RANK:KR_OFF], kvlg_ref[...]).astype(BF16)
    kvraw = _dot(ckv, kv_up_ref[...])
    kr = proj[:, KR_OFF:KR_OFF + LANES]
    ss_kr = jnp.sum(kr * kr, axis=-1, keepdims=True)
    kr_rot = _rope_slab(kr, proj[:, KR_OFF + LANES:DIL_OFF], (tk_ref[0], tk_ref[1]))
    gkn = gkn_ref[...]
    for hd in range(MLA_HEADS):
        slab = kvraw[:, hd * LANES:(hd + 1) * LANES]
        ss = jnp.sum(slab * slab, axis=-1, keepdims=True) + ss_kr
        rinv = lax.rsqrt(ss * (1.0 / MLA_QK) + RMS_EPS)
        ka_ref[0, :, hd * LANES:(hd + 1) * LANES] = ((slab * gkn + kr_rot) * rinv).astype(BF16)
    va_ref[0] = kvraw[:, MLA_HEADS * LANES:].astype(BF16)

    tqb = (tqb_ref[0], tqb_ref[1])
    tkb = (tkb_ref[0], tkb_ref[1])
    part_off = DIL_OFF + 3 * DIL_W
    for j in range(DIL_W // LANES):
        cq0, ck0 = j * LANES, DIL_W + j * LANES
        qb_ref[0, :, j * LANES:(j + 1) * LANES] = _norm_rope_pair(
            proj[:, DIL_OFF + cq0:DIL_OFF + cq0 + LANES],
            proj[:, part_off + cq0:part_off + cq0 + LANES], tqb).astype(BF16)
        kb_ref[0, :, j * LANES:(j + 1) * LANES] = _norm_rope_pair(
            proj[:, DIL_OFF + ck0:DIL_OFF + ck0 + LANES],
            proj[:, part_off + ck0:part_off + ck0 + LANES], tkb).astype(BF16)
    vb_ref[0] = proj[:, DIL_OFF + 2 * DIL_W:DIL_OFF + 3 * DIL_W].astype(BF16)


def _even_in(x, mod_l, ng, wts, tabs):
    b, t, _ = x.shape
    tm = min(ROW_TILE, t)
    w_in, qlg, q_up, kvlg, kv_up, gkn = wts
    row = lambda c: pl.BlockSpec((1, tm, c), lambda i, j: (i, j, 0))
    tab = pl.BlockSpec((2, tm, LANES), lambda i, j: (0, j, 0))
    hc = DIL_HEADS * HEAD_DIM
    outs = [jax.ShapeDtypeStruct((b, t, c), BF16)
            for c in (MLA_HEADS * LANES, MLA_HEADS * LANES, MLA_HEADS * MLA_V, hc, hc, hc)]
    return pl.pallas_call(
        _even_in_kernel,
        grid=(b, t // tm),
        in_specs=[
            row(D_MODEL),
            pl.BlockSpec((1, 6, D_MODEL), lambda i, j: (i, 0, 0)),
            _const_spec(ng.shape), _const_spec(w_in.shape), _const_spec(qlg.shape),
            _const_spec(q_up.shape), _const_spec(kvlg.shape), _const_spec(kv_up.shape),
            _const_spec(gkn.shape), tab, tab, tab, tab,
        ],
        out_specs=[row(s.shape[2]) for s in outs],
        out_shape=outs,
        compiler_params=_cparams("parallel", "parallel"),
        name="even_in",
    )(x, mod_l, ng, w_in, qlg, q_up, kvlg, kv_up, gkn, *tabs)


def _mla_kernel(q_ref, k_ref, v_ref, o_ref, vaug_s):
    lo = _lane_lo()

    @pl.when(pl.program_id(2) == 0)
    def _():
        v = v_ref[0]
        one = jnp.ones((), BF16)
        vaug_s[0] = jnp.where(lo, v, one)
        vaug_s[1] = jnp.where(lo, one, v)

    sub = q_ref.shape[1] // MLA_SUB_TILES
    for si in range(MLA_SUB_TILES):
        rows = slice(si * sub, (si + 1) * sub)
        outs = []
        for hd in range(2):
            q = q_ref[0, rows, hd * LANES:(hd + 1) * LANES]
            k = k_ref[0, :, hd * LANES:(hd + 1) * LANES]
            s = _dot_t(q, k)
            m = jnp.max(s, axis=-1, keepdims=True)
            p = jnp.exp2((s - m).astype(BF16))
            outs.append(_dot(p, vaug_s[hd]))
        num = jnp.where(lo, outs[0], outs[1])
        den = jnp.where(lo, pltpu.roll(outs[0], HEAD_DIM, 1), pltpu.roll(outs[1], HEAD_DIM, 1))
        o_ref[0, rows, :] = (num * (1.0 / den)).astype(BF16)


def _mla_attention(q, k, v):
    b, t, _ = q.shape
    tq = min(MLA_Q_TILE, t)
    return pl.pallas_call(
        _mla_kernel,
        grid=(b, MLA_HEADS // 2, t // tq),
        in_specs=[
            pl.BlockSpec((1, tq, 2 * LANES), lambda i, c, j: (i, j, c)),
            pl.BlockSpec((1, t, 2 * LANES), lambda i, c, j: (i, 0, c)),
            pl.BlockSpec((1, t, LANES), lambda i, c, j: (i, 0, c)),
        ],
        out_specs=pl.BlockSpec((1, tq, LANES), lambda i, c, j: (i, j, c)),
        out_shape=jax.ShapeDtypeStruct((b, t, MLA_HEADS * MLA_V), BF16),
        scratch_shapes=[pltpu.VMEM((2, t, LANES), BF16)],
        compiler_params=_cparams("parallel", "parallel", "arbitrary"),
        name="mla_attn",
    )(q, k, v)


def _band_chain(qs_list, kw, vw, mask, tq):
    lo = _lane_lo()
    m_lo = jnp.where(lo, 1.0, 0.0).astype(BF16)
    m_hi = jnp.where(lo, 0.0, 1.0).astype(BF16)
    parts = []
    for qs in qs_list:
        parts += [qs * m_lo, qs * m_hi]
    s = mask(_dot_t(jnp.concatenate(parts, axis=0), kw))
    m = jnp.max(s, axis=-1, keepdims=True)
    p = jnp.exp2(s - m)
    l = jnp.sum(p, axis=-1, keepdims=True)
    o = _dot(p.astype(BF16), vw) * (1.0 / l)
    lse = (m + jnp.log2(l)) * math.log(2.0)
    res = []
    for qi in range(len(qs_list)):
        a, b = 2 * qi * tq, (2 * qi + 1) * tq
        res.append((jnp.where(lo, o[a:a + tq], o[b:b + tq]),
                    jnp.where(lo, lse[a:a + tq], lse[b:b + tq])))
    return res


def _band_rel(rows, tq, win, half):
    return (lax.rem(lax.broadcasted_iota(jnp.int32, (rows, win), 0), tq)
            - lax.broadcasted_iota(jnp.int32, (rows, win), 1)) + half


def _band_valid(relh, off, half):
    return lax.bitcast_convert_type(relh + off, jnp.uint32) <= jnp.uint32(2 * half)


def _win_kernel(q_ref, k_ref, v_ref, sink_ref, o_ref, *, seq, tq, win):
    half, nq = WIN_HALF, 2
    relh = _band_rel(2 * nq * tq, tq, win, half)

    def tile(ti, carry):
        q0 = pl.multiple_of(ti * tq, tq)
        ws = 0 if win == seq else pl.multiple_of(jnp.clip(q0 - half, 0, seq - win), HEAD_DIM)
        valid = _band_valid(relh, q0 - ws, half)
        qs = [q_ref[0, pl.ds(q0, tq), qi * LANES:(qi + 1) * LANES] for qi in range(nq)]
        res = _band_chain(qs, k_ref[0, pl.ds(ws, win), :], v_ref[0, pl.ds(ws, win), :],
                          lambda s: jnp.where(valid, s, NEG_INF), tq)
        for qi, (o_slab, lse_slab) in enumerate(res):
            keep = _sigmoid(lse_slab - sink_ref[:, qi * LANES:(qi + 1) * LANES])
            o_ref[0, pl.ds(q0, tq), qi * LANES:(qi + 1) * LANES] = (o_slab * keep).astype(BF16)
        return carry

    n_t = seq // tq
    lax.fori_loop(0, n_t, tile, 0, unroll=min(BAND_UNROLL, n_t))


def _win_attention(q, k, v, sink):
    b, seq, _ = q.shape
    tq = min(BAND_Q_TILE, seq)
    win = min(tq + 2 * WIN_HALF, seq)
    assert tq >= WIN_HALF or win == seq
    kern = functools.partial(_win_kernel, seq=seq, tq=tq, win=win)
    return pl.pallas_call(
        kern,
        grid=(b, WIN_KV_HEADS),
        in_specs=[
            pl.BlockSpec((1, seq, 2 * LANES), lambda i, c: (i, 0, c)),
            pl.BlockSpec((1, seq, LANES), lambda i, c: (i, 0, c)),
            pl.BlockSpec((1, seq, LANES), lambda i, c: (i, 0, c)),
            pl.BlockSpec((1, 2 * LANES), lambda i, c: (0, c)),
        ],
        out_specs=pl.BlockSpec((1, seq, 2 * LANES), lambda i, c: (i, 0, c)),
        out_shape=jax.ShapeDtypeStruct(q.shape, BF16),
        compiler_params=_cparams("parallel", "parallel"),
        name="win_attn",
    )(q, k, v, sink)


def _dil_kernel(q_ref, k_ref, v_ref, o_ref, qf, kf, vf, qd, kd, vd, acc, lse, of, *, seq):
    qf[...] = q_ref[0].astype(F32)
    kf[...] = k_ref[0].astype(F32)
    vf[...] = v_ref[0].astype(F32)
    for pi in (1, 2):
        d = DIL_PATTERNS[pi][1]
        ln = seq // d
        for r in range(d):
            for src, dst in ((qf, qd), (kf, kd), (vf, vd)):
                dst[pi - 1, r * ln:(r + 1) * ln, :] = src[pl.ds(r, ln, stride=d), :].astype(BF16)

    for pi, (window, d) in enumerate(DIL_PATTERNS):
        ln = seq // d
        half = window // (2 * d)
        tq = min(BAND_Q_TILE, ln)
        win = min(tq + 2 * half, ln)
        n_t = ln // tq
        align = min(tq, HEAD_DIM)
        rel = _band_rel(2 * tq, tq, win, half)
        if pi == 0:
            load = lambda ref, row, n: ref[0, pl.ds(row, n), :]
            srcs = (q_ref, k_ref, v_ref)
        else:
            load = lambda ref, row, n, pi=pi: ref[pi - 1, pl.ds(row, n), :]
            srcs = (qd, kd, vd)

        def body(i, carry, pi=pi, ln=ln, half=half, tq=tq, win=win, n_t=n_t, align=align,
                 rel=rel, load=load, srcs=srcs):
            r = i // n_t
            q0 = (i - r * n_t) * tq
            ws = 0 if win == ln else jnp.clip(q0 - half, 0, ln - win)
            valid = _band_valid(rel, q0 - ws, half)
            rq = pl.multiple_of(r * ln + q0, align)
            rk = pl.multiple_of(r * ln + ws, align)
            ((o_slab, lse_slab),) = _band_chain(
                [load(srcs[0], rq, tq)], load(srcs[1], rk, win), load(srcs[2], rk, win),
                lambda s: jnp.where(valid, s, NEG_INF), tq)
            acc[pi, pl.ds(rq, tq), :] = o_slab
            lse[pi, pl.ds(rq, tq), :] = lse_slab
            return carry

        lax.fori_loop(0, seq // tq, body, 0, unroll=min(8, seq // tq))

    d2, d3 = DIL_PATTERNS[1][1], DIL_PATTERNS[2][1]
    l2n, l3n = seq // d2, seq // d3
    for r in range(d3):
        s1 = pl.ds(r, l3n, stride=d3)
        s2 = pl.ds((r % d2) * l2n + r // d2, l3n, stride=d3 // d2)
        s3 = pl.ds(r * l3n, l3n)
        l1, l2, l3 = lse[0, s1, :], lse[1, s2, :], lse[2, s3, :]
        m = jnp.maximum(jnp.maximum(l1, l2), l3)
        e1, e2, e3 = jnp.exp(l1 - m), jnp.exp(l2 - m), jnp.exp(l3 - m)
        of[s1, :] = ((e1 * acc[0, s1, :] + e2 * acc[1, s2, :] + e3 * acc[2, s3, :])
                     * (1.0 / (e1 + e2 + e3)))
    o_ref[0] = of[...].astype(BF16)


def _dil_attention(q, k, v):
    b, seq, c = q.shape
    blk = pl.BlockSpec((1, seq, LANES), lambda i, j: (i, 0, j))
    f32s = pltpu.VMEM((seq, LANES), F32)
    res = pltpu.VMEM((2, seq, LANES), BF16)
    pat = pltpu.VMEM((len(DIL_PATTERNS), seq, LANES), F32)
    return pl.pallas_call(
        functools.partial(_dil_kernel, seq=seq),
        grid=(b, c // LANES),
        in_specs=[blk, blk, blk],
        out_specs=blk,
        out_shape=jax.ShapeDtypeStruct(q.shape, BF16),
        scratch_shapes=[f32s, f32s, f32s, res, res, res, pat, pat, f32s],
        compiler_params=_cparams("parallel", "parallel"),
        name="dil_attn",
    )(q, k, v)


def _even_out_kernel(x_ref, mod_ref, oa_ref, ob_ref, w_ref, y_ref):
    half = w_ref.shape[0] // 2
    mix = _dot(oa_ref[0], w_ref[:half, :]) + _dot(ob_ref[0], w_ref[half:, :])
    y_ref[0] = x_ref[0] + mod_ref[0, 2:3, :] * mix


def _even_out(x, mod_l, oa, ob, w_out):
    b, t, _ = x.shape
    tm = min(ROW_TILE, t)
    row = lambda c: pl.BlockSpec((1, tm, c), lambda i, j: (i, j, 0))
    return pl.pallas_call(
        _even_out_kernel,
        grid=(b, t // tm),
        in_specs=[row(D_MODEL), pl.BlockSpec((1, 6, D_MODEL), lambda i, j: (i, 0, 0)),
                  row(oa.shape[2]), row(ob.shape[2]), _const_spec(w_out.shape)],
        out_specs=row(D_MODEL),
        out_shape=jax.ShapeDtypeStruct(x.shape, F32),
        compiler_params=_cparams("parallel", "parallel"),
        name="even_out",
    )(x, mod_l, oa, ob, w_out)


def _dense_ffn_kernel(x_ref, mod_ref, ng_ref, wu_ref, wd_ref, y_ref):
    x = x_ref[0]
    h = _norm_mod(x, ng_ref[...], mod_ref[0, 4:5, :], mod_ref[0, 3:4, :]).astype(BF16)
    ff = wd_ref.shape[0]
    acc = None
    c0 = 0
    for cw in DENSE_FF_CHUNKS:
        g = _dot(h, wu_ref[:, c0:c0 + cw])
        u = _dot(h, wu_ref[:, ff + c0:ff + c0 + cw])
        a = (g * _sigmoid(g) * u).astype(BF16)
        d = _dot(a, wd_ref[c0:c0 + cw, :])
        acc = d if acc is None else acc + d
        c0 += cw
    y_ref[0] = x + mod_ref[0, 5:6, :] * acc


def _dense_ffn(x, mod_l, ng, w_up, w_down):
    b, t, _ = x.shape
    tm = min(ROW_TILE, t)
    row = pl.BlockSpec((1, tm, D_MODEL), lambda i, j: (i, j, 0))
    return pl.pallas_call(
        _dense_ffn_kernel,
        grid=(b, t // tm),
        in_specs=[row, pl.BlockSpec((1, 6, D_MODEL), lambda i, j: (i, 0, 0)),
                  _const_spec(ng.shape), _const_spec(w_up.shape), _const_spec(w_down.shape)],
        out_specs=row,
        out_shape=jax.ShapeDtypeStruct(x.shape, F32),
        compiler_params=_cparams("parallel", "parallel"),
        name="dense_ffn",
    )(x, mod_l, ng, w_up, w_down)


def _odd_in_kernel(x_ref, mod_ref, ng_ref, w_in_ref, tq_ref, tk_ref, q_ref, k_ref, v_ref):
    h = _norm_mod(x_ref[0], ng_ref[...], mod_ref[0, 1:2, :], mod_ref[0, 0:1, :]).astype(BF16)
    proj = _dot(h, w_in_ref[...])
    tq = (tq_ref[0], tq_ref[1])
    tk = (tk_ref[0], tk_ref[1])
    nq = WIN_Q_HEADS * HEAD_DIM // LANES
    part = (nq + 2 * WIN_KV_HEADS) * LANES
    for j in range(nq):
        q_ref[0, :, j * LANES:(j + 1) * LANES] = _norm_rope_pair(
            proj[:, j * LANES:(j + 1) * LANES],
            proj[:, part + j * LANES:part + (j + 1) * LANES], tq).astype(BF16)
    for j in range(WIN_KV_HEADS):
        c0 = (nq + j) * LANES
        k_ref[0, :, j * LANES:(j + 1) * LANES] = _norm_rope_pair(
            proj[:, c0:c0 + LANES], proj[:, part + c0:part + c0 + LANES], tk).astype(BF16)
    v_ref[0] = proj[:, (nq + WIN_KV_HEADS) * LANES:part].astype(BF16)


def _odd_in(x, mod_l, ng, w_in, tabs):
    b, t, _ = x.shape
    tm = min(ROW_TILE, t)
    row = lambda c: pl.BlockSpec((1, tm, c), lambda i, j: (i, j, 0))
    tab = pl.BlockSpec((2, tm, LANES), lambda i, j: (0, j, 0))
    widths = (WIN_Q_HEADS * HEAD_DIM, WIN_KV_HEADS * LANES, WIN_KV_HEADS * LANES)
    return pl.pallas_call(
        _odd_in_kernel,
        grid=(b, t // tm),
        in_specs=[row(D_MODEL), pl.BlockSpec((1, 6, D_MODEL), lambda i, j: (i, 0, 0)),
                  _const_spec(ng.shape), _const_spec(w_in.shape), tab, tab],
        out_specs=[row(c) for c in widths],
        out_shape=[jax.ShapeDtypeStruct((b, t, c), BF16) for c in widths],
        compiler_params=_cparams("parallel", "parallel"),
        name="odd_in",
    )(x, mod_l, ng, w_in, *tabs)


def _odd_out_kernel(x_ref, mod_ref, o_ref, w_ref, y_ref):
    y_ref[0] = x_ref[0] + mod_ref[0, 2:3, :] * _dot(o_ref[0], w_ref[...])


def _odd_out(x, mod_l, o, w_out):
    b, t, _ = x.shape
    tm = min(ROW_TILE, t)
    row = pl.BlockSpec((1, tm, D_MODEL), lambda i, j: (i, j, 0))
    return pl.pallas_call(
        _odd_out_kernel,
        grid=(b, t // tm),
        in_specs=[row, pl.BlockSpec((1, 6, D_MODEL), lambda i, j: (i, 0, 0)), row,
                  _const_spec(w_out.shape)],
        out_specs=row,
        out_shape=jax.ShapeDtypeStruct(x.shape, F32),
        compiler_params=_cparams("parallel", "parallel"),
        name="odd_out",
    )(x, mod_l, o, w_out)


def _router_kernel(x_ref, mod_ref, ng_ref, wr_ref, h_ref, gate_ref, sel_ref):
    h = _norm_mod(x_ref[0], ng_ref[...], mod_ref[0, 4:5, :], mod_ref[0, 3:4, :])
    w = _pack_row(h)
    h_ref[0, 0] = w[:, :PLANE_W]
    h_ref[1, 0] = w[:, PLANE_W:]
    lane = lax.broadcasted_iota(jnp.int32, (1, LANES), 1).astype(F32)
    logits = jnp.where(lane < N_EXPERTS, _dot3(h, wr_ref[...]), NEG_INF)
    e = jnp.exp(logits - jnp.max(logits, axis=-1, keepdims=True))
    p = e * (1.0 / jnp.sum(e, axis=-1, keepdims=True))
    p = jnp.where(lane < N_EXPERTS, p, -1.0)
    m1 = jnp.max(p, axis=-1, keepdims=True)
    i1 = jnp.min(jnp.where(p == m1, lane, float(LANES)), axis=-1, keepdims=True)
    s1 = lane == i1
    p2 = jnp.where(s1, -1.0, p)
    m2 = jnp.max(p2, axis=-1, keepdims=True)
    i2 = jnp.min(jnp.where(p2 == m2, lane, float(LANES)), axis=-1, keepdims=True)
    s2 = lane == i2
    inv = 1.0 / (m1 + m2)
    gate_ref[0] = jnp.where(s1, m1 * inv, 0.0) + jnp.where(s2, m2 * inv, 0.0)
    sel_ref[0] = jnp.where(s1, 1.0, 0.0) + jnp.where(s2, 1.0, 0.0)


def _router(x, mod_l, ng, wr):
    b, t, _ = x.shape
    tm = min(ROW_TILE, t)
    row = lambda c: pl.BlockSpec((1, tm, c), lambda i, j: (i, j, 0))
    return pl.pallas_call(
        _router_kernel,
        grid=(b, t // tm),
        in_specs=[row(D_MODEL), pl.BlockSpec((1, 6, D_MODEL), lambda i, j: (i, 0, 0)),
                  _const_spec(ng.shape), _const_spec(wr.shape)],
        out_specs=[pl.BlockSpec((2, 1, tm, PLANE_W), lambda i, j: (0, i, j, 0)), row(LANES), row(LANES)],
        out_shape=[jax.ShapeDtypeStruct((2, b, t, PLANE_W), jnp.uint32),
                   jax.ShapeDtypeStruct((b, t, LANES), F32),
                   jax.ShapeDtypeStruct((b, t, LANES), F32)],
        compiler_params=_cparams("parallel", "parallel"),
        name="moe_router",
    )(x, mod_l, ng, wr)


SC_WINDOW = 128
SC_COLS = PLANE_W


def _sc_mesh():
    return plsc.VectorSubcoreMesh(core_axis_name="core", subcore_axis_name="subcore")


def _scatter_rows_sc(src, pos_a, pos_b, n_out):
    n = src.shape[0]

    @pl.kernel(out_type=jax.ShapeDtypeStruct((n_out, SC_COLS), src.dtype), mesh=_sc_mesh(),
               scratch_types=[])
    def scatter(src_hbm, ia_hbm, ib_hbm, dst_hbm):
        def body(rows_vmem, ia_vmem, ib_vmem):
            pltpu.sync_copy(rows_vmem, dst_hbm.at[ia_vmem.at[0]])
            pltpu.sync_copy(rows_vmem, dst_hbm.at[ib_vmem.at[0]])

        idx_spec = pl.BlockSpec((1, SC_WINDOW), lambda i: (0, i))
        pltpu.emit_pipeline(
            body,
            grid=(n // SC_WINDOW,),
            in_specs=[pl.BlockSpec((SC_WINDOW, SC_COLS), lambda i: (i, 0)), idx_spec, idx_spec],
            out_specs=[],
            core_axis_name=("core", "subcore"),
            dimension_semantics=(pltpu.PARALLEL,),
        )(src_hbm, ia_hbm, ib_hbm)

    return scatter(src, pos_a.reshape(1, n), pos_b.reshape(1, n))


def _gather_rows_sc(src, pos):
    n = pos.shape[0]

    @pl.kernel(out_type=jax.ShapeDtypeStruct((n, SC_COLS), src.dtype), mesh=_sc_mesh(),
               scratch_types=[])
    def gather(src_hbm, i_hbm, dst_hbm):
        def body(i_vmem, rows_vmem):
            pltpu.sync_copy(src_hbm.at[i_vmem.at[0]], rows_vmem)

        pltpu.emit_pipeline(
            body,
            grid=(n // SC_WINDOW,),
            in_specs=[pl.BlockSpec((1, SC_WINDOW), lambda i: (0, i))],
            out_specs=[pl.BlockSpec((SC_WINDOW, SC_COLS), lambda i: (i, 0))],
            core_axis_name=("core", "subcore"),
            dimension_semantics=(pltpu.PARALLEL,),
        )(i_hbm, dst_hbm)

    return gather(src, pos.reshape(1, n))


def _expert_kernel(te_ref, nu_ref, x0_ref, x1_ref, wg_ref, wu_ref, wd_ref, ys_ref, xb_ref, acc_ref):
    i, j = pl.program_id(0), pl.program_id(1)
    used = i < nu_ref[0]

    @pl.when(jnp.logical_and(used, j == 0))
    def _():
        for pi, x_ref in enumerate((x0_ref, x1_ref)):
            lo, hi = _unpack_row(x_ref[0])
            xb_ref[:, pi * PLANE_W:(pi + 1) * PLANE_W] = lo.astype(BF16)
            xb_ref[:, HALF_D + pi * PLANE_W:HALF_D + (pi + 1) * PLANE_W] = hi.astype(BF16)
        acc_ref[...] = jnp.zeros_like(acc_ref)

    @pl.when(used)
    def _():
        xb = xb_ref[...]
        g = _dot(xb, wg_ref[0])
        u = _dot(xb, wu_ref[0])
        a = (g * _sigmoid(g) * u).astype(BF16)
        acc_ref[...] += _dot(a, wd_ref[0])

    last = j == pl.num_programs(1) - 1

    @pl.when(jnp.logical_and(used, last))
    def _():
        w = _pack_row(acc_ref[...])
        ys_ref[0] = w[:, :PLANE_W]
        ys_ref[1] = w[:, PLANE_W:]

    @pl.when(jnp.logical_and(jnp.logical_not(used), last))
    def _():
        ys_ref[...] = jnp.zeros_like(ys_ref)


def _experts(xs, tile_expert, n_used, w_up, w_down):
    p = xs.shape[1]
    nck = FF_EXPERT // MOE_FF_CHUNK
    tm = MOE_ROW_TILE

    def chunk(i, j, te, nu):
        return jnp.where(i < nu[0], j, nck - 1)

    grid_spec = pltpu.PrefetchScalarGridSpec(
        num_scalar_prefetch=2,
        grid=(p // tm, nck),
        in_specs=[
            pl.BlockSpec((1, tm, PLANE_W), lambda i, j, te, nu: (0, i, 0)),
            pl.BlockSpec((1, tm, PLANE_W), lambda i, j, te, nu: (1, i, 0)),
            pl.BlockSpec((1, D_MODEL, MOE_FF_CHUNK), lambda i, j, te, nu: (te[i], 0, chunk(i, j, te, nu))),
            pl.BlockSpec((1, D_MODEL, MOE_FF_CHUNK),
                         lambda i, j, te, nu: (te[i], 0, nck + chunk(i, j, te, nu))),
            pl.BlockSpec((1, MOE_FF_CHUNK, D_MODEL), lambda i, j, te, nu: (te[i], chunk(i, j, te, nu), 0)),
        ],
        out_specs=pl.BlockSpec((2, tm, PLANE_W), lambda i, j, te, nu: (0, i, 0)),
        scratch_shapes=[pltpu.VMEM((tm, D_MODEL), BF16), pltpu.VMEM((tm, D_MODEL), F32)],
    )
    return pl.pallas_call(
        _expert_kernel,
        grid_spec=grid_spec,
        out_shape=jax.ShapeDtypeStruct((2, p, PLANE_W), jnp.uint32),
        compiler_params=_cparams("arbitrary", "arbitrary"),
        name="moe_experts",
    )(tile_expert, n_used, xs, xs, w_up, w_up, w_down)


def _combine_kernel(x_ref, mod_ref, g_ref, a0_ref, b0_ref, a1_ref, b1_ref, o_ref):
    g = g_ref[0]
    ga, gb = g[:, 0:1], g[:, 1:2]
    for pi, (a_ref, b_ref) in enumerate(((a0_ref, b0_ref), (a1_ref, b1_ref))):
        a_lo, a_hi = _unpack_row(a_ref[0, 0, 0])
        b_lo, b_hi = _unpack_row(b_ref[0, 0, 0])
        for c0, ff in ((pi * PLANE_W, ga * a_lo + gb * b_lo),
                       (HALF_D + pi * PLANE_W, ga * a_hi + gb * b_hi)):
            cols = slice(c0, c0 + PLANE_W)
            o_ref[0, :, cols] = x_ref[0, :, cols] + mod_ref[0, 5:6, cols] * ff


def _combine(x, mod_l, gate_ab, yg):
    b, t, _ = x.shape
    tm = min(ROW_TILE, t)
    row = pl.BlockSpec((1, tm, D_MODEL), lambda i, j: (i, j, 0))
    part = lambda pi, ab: pl.BlockSpec((1, 1, 1, tm, PLANE_W), lambda i, j: (pi, ab, i, j, 0))
    return pl.pallas_call(
        _combine_kernel,
        grid=(b, t // tm),
        in_specs=[row, pl.BlockSpec((1, 6, D_MODEL), lambda i, j: (i, 0, 0)),
                  pl.BlockSpec((1, tm, 2), lambda i, j: (i, j, 0)),
                  part(0, 0), part(0, 1), part(1, 0), part(1, 1)],
        out_specs=row,
        out_shape=jax.ShapeDtypeStruct(x.shape, F32),
        compiler_params=_cparams("parallel", "parallel"),
        name="moe_combine",
    )(x, mod_l, gate_ab, yg, yg, yg, yg)


def _moe(x, mod_l, ng, wr, w_up, w_down):
    b, t, _ = x.shape
    n = b * t
    h, gates, sel = _router(x, mod_l, ng, wr)
    gates = gates.reshape(n, LANES)[:, :N_EXPERTS]
    sel = sel.reshape(n, LANES)[:, :N_EXPERTS] > 0.5

    tm = MOE_ROW_TILE
    n_tiles = 2 * n // tm + N_EXPERTS
    p = n_tiles * tm
    sel_i = sel.astype(jnp.int32)
    rank = jnp.cumsum(sel_i, axis=0) - sel_i
    counts = jnp.sum(sel_i, axis=0)
    padded = (counts + tm - 1) // tm * tm
    ends = jnp.cumsum(padded)
    pos = (ends - padded)[None, :] + rank
    tile_start = jnp.arange(n_tiles, dtype=jnp.int32) * tm
    tile_expert = jnp.minimum(jnp.sum((tile_start[:, None] >= ends[None, :]).astype(jnp.int32), axis=1),
                              N_EXPERTS - 1)
    n_used = (ends[-1] // tm).astype(jnp.int32).reshape(1)
    order = jnp.cumsum(sel_i, axis=1)
    first, second = sel & (order == 1), sel & (order == 2)
    pos_a = jnp.sum(jnp.where(first, pos, 0), axis=1)
    pos_b = jnp.sum(jnp.where(second, pos, 0), axis=1)
    gate_ab = jnp.stack([jnp.sum(jnp.where(first, gates, 0.0), axis=1),
                         jnp.sum(jnp.where(second, gates, 0.0), axis=1)], axis=1)

    xs = _scatter_rows_sc(h.reshape(2 * n, PLANE_W), jnp.concatenate([pos_a, pos_a + p]),
                          jnp.concatenate([pos_b, pos_b + p]), 2 * p)
    ys = _experts(xs.reshape(2, p, PLANE_W), tile_expert, n_used, w_up, w_down)
    yg = _gather_rows_sc(ys.reshape(2 * p, PLANE_W),
                         jnp.concatenate([pos_a, pos_b, pos_a + p, pos_b + p]))
    return _combine(x, mod_l, gate_ab.reshape(b, t, 2), yg.reshape(2, 2, b, t, PLANE_W))


def _rope_tables(t, dim):
    inv = ROPE_THETA ** (-jnp.arange(0, dim, 2, dtype=F32) / dim)
    ang = jnp.arange(t, dtype=F32)[:, None] * inv[None, :]
    return jnp.cos(ang), jnp.sin(ang)


def _rope_lane_tables(cos, sin, gain, offs, scale):
    t, hw = cos.shape
    c, s = jnp.ones((t, LANES), F32), jnp.zeros((t, LANES), F32)
    g_up, g_dn = jnp.roll(gain, hw), jnp.roll(gain, -hw)
    g_partner = jnp.zeros((LANES,), F32)
    for o in offs:
        c = c.at[:, o:o + hw].set(cos).at[:, o + hw:o + 2 * hw].set(cos)
        s = s.at[:, o:o + hw].set(-sin).at[:, o + hw:o + 2 * hw].set(sin)
        g_partner = g_partner.at[o:o + hw].set(g_dn[o:o + hw]).at[o + hw:o + 2 * hw].set(
            g_up[o + hw:o + 2 * hw])
    return jnp.stack([c * gain, s * g_partner]) * scale


def _swap_halves(w, width):
    d, c = w.shape
    return w.reshape(d, c // width, 2, width // 2)[:, :, ::-1, :].reshape(d, c)


def _pad_lanes(v, width=LANES):
    return jnp.zeros((width,), F32).at[:v.shape[0]].set(v)


def _even_weights(w_in, qlg, q_up, kvlg, kv_up):
    w_kr = w_in[:, KR_OFF:KR_OFF + MLA_ROPE]
    slab = lambda w: jnp.zeros((D_MODEL, LANES), F32).at[:, MLA_NOPE:MLA_QK].set(w)
    w_dil = w_in[:, KR_OFF + MLA_ROPE:]
    n_qk = 2 * DIL_HEADS * HEAD_DIM
    w_in_p = jnp.concatenate([w_in[:, :KR_OFF], slab(w_kr), slab(_swap_halves(w_kr, MLA_ROPE)),
                              w_dil, _swap_halves(w_dil[:, :n_qk], HEAD_DIM)], axis=1)
    q3 = q_up.reshape(MLA_Q_RANK, MLA_HEADS, MLA_QK)
    q_rope_p = _swap_halves(q3[:, :, MLA_NOPE:].reshape(MLA_Q_RANK, -1), MLA_ROPE)
    q_part = jnp.pad(q_rope_p.reshape(MLA_Q_RANK, MLA_HEADS, MLA_ROPE),
                     ((0, 0), (0, 0), (MLA_NOPE, LANES - MLA_QK)))
    q_up_p = jnp.concatenate([jnp.pad(q3, ((0, 0), (0, 0), (0, LANES - MLA_QK))), q_part],
                             axis=1).reshape(MLA_Q_RANK, 2 * MLA_HEADS * LANES)
    kv3 = kv_up.reshape(MLA_KV_RANK, MLA_HEADS, MLA_NOPE + MLA_V)
    k_p = jnp.pad(kv3[:, :, :MLA_NOPE], ((0, 0), (0, 0), (0, LANES - MLA_NOPE)))
    kv_up_p = jnp.concatenate([k_p.reshape(MLA_KV_RANK, -1),
                               kv3[:, :, MLA_NOPE:].reshape(MLA_KV_RANK, -1)], axis=1)
    return (w_in_p.astype(BF16), qlg.reshape(1, -1), q_up_p.astype(BF16), kvlg.reshape(1, -1),
            kv_up_p.astype(BF16))


def _odd_weights(w_in):
    nq = WIN_Q_HEADS * HEAD_DIM
    nkv = WIN_KV_HEADS * HEAD_DIM
    dup = lambda w: jnp.concatenate([w.reshape(D_MODEL, WIN_KV_HEADS, 1, HEAD_DIM)] * 2,
                                    axis=2).reshape(D_MODEL, 2 * nkv)
    w_q, w_k = w_in[:, :nq], dup(w_in[:, nq:nq + nkv])
    return jnp.concatenate([w_q, w_k, dup(w_in[:, nq + nkv:]), _swap_halves(w_q, HEAD_DIM),
                            _swap_halves(w_k, HEAD_DIM)], axis=1).astype(BF16)


def _trunk(x, c, p):
    b, t, _ = x.shape
    mod = _ada_mod(c, p["ada_w"], p["ada_b"])
    cos_r, sin_r = _rope_tables(t, MLA_ROPE)
    cos_h, sin_h = _rope_tables(t, HEAD_DIM)
    pair_offs = (0, HEAD_DIM)
    for l in range(DEPTH):
        i = l // 2
        mod_l = mod[l]
        ng = p["norm_g"][l]
        if l % 2 == 0:
            g_q, g_k = p["even_mla_qk_g"][i, 0], p["even_mla_qk_g"][i, 1]
            g_qb, g_kb = p["even_dil_qk_g"][i, 0], p["even_dil_qk_g"][i, 1]
            tabs = (
                _rope_lane_tables(cos_r, sin_r, _pad_lanes(g_q), (MLA_NOPE,),
                                  MLA_QK ** -0.5 * math.log2(math.e)),
                _rope_lane_tables(cos_r, sin_r, _pad_lanes(g_k).at[:MLA_NOPE].set(0.0), (MLA_NOPE,), 1.0),
                _rope_lane_tables(cos_h, sin_h, jnp.tile(g_qb, 2), pair_offs,
                                  HEAD_DIM ** -0.5 * math.log2(math.e)),
                _rope_lane_tables(cos_h, sin_h, jnp.tile(g_kb, 2), pair_offs, 1.0),
            )
            gkn = _pad_lanes(g_k[:MLA_NOPE]).reshape(1, LANES)
            wts = _even_weights(p["even_w_in"][i], p["even_q_latent_g"][i], p["even_q_up"][i],
                                p["even_kv_latent_g"][i], p["even_kv_up"][i]) + (gkn,)
            qa, ka, va, qb, kb, vb = _even_in(x, mod_l, ng[0:1], wts, tabs)
            oa = _mla_attention(qa, ka, va)
            ob = _dil_attention(qb, kb, vb)
            x = _even_out(x, mod_l, oa, ob, p["even_w_out"][i].astype(BF16))
            x = _dense_ffn(x, mod_l, ng[1:2], p["dense_w_up"][i].astype(BF16),
                           p["dense_w_down"][i].astype(BF16))
        else:
            g_q, g_k = p["odd_qk_g"][i, 0], p["odd_qk_g"][i, 1]
            tabs = (
                _rope_lane_tables(cos_h, sin_h, jnp.tile(g_q, 2), pair_offs,
                                  HEAD_DIM ** -0.5 * math.log2(math.e)),
                _rope_lane_tables(cos_h, sin_h, jnp.tile(g_k, 2), pair_offs, 1.0),
            )
            q, k, v = _odd_in(x, mod_l, ng[0:1], _odd_weights(p["odd_w_in"][i]), tabs)
            sink = jnp.repeat(p["odd_sink"][i], HEAD_DIM).reshape(1, -1)
            o = _win_attention(q, k, v, sink)
            x = _odd_out(x, mod_l, o, p["odd_w_out"][i].astype(BF16))
            wr = jnp.zeros((D_MODEL, LANES), F32).at[:, :N_EXPERTS].set(p["moe_router"][i])
            x = _moe(x, mod_l, ng[1:2], wr, p["moe_w_up"][i].astype(BF16),
                     p["moe_w_down"][i].astype(BF16))
    return x


def kernel(x_prompt, x_sample, c_prompt, c_sample, ada_w, ada_b, norm_g, even_w_in, even_q_latent_g, even_q_up, even_kv_latent_g, even_kv_up, even_mla_qk_g, even_dil_qk_g, even_w_out, odd_w_in, odd_qk_g, odd_sink, odd_w_out, dense_w_up, dense_w_down, moe_router, moe_w_up, moe_w_down):
    p = dict(ada_w=ada_w, ada_b=ada_b, norm_g=norm_g, even_w_in=even_w_in,
             even_q_latent_g=even_q_latent_g, even_q_up=even_q_up,
             even_kv_latent_g=even_kv_latent_g, even_kv_up=even_kv_up,
             even_mla_qk_g=even_mla_qk_g, even_dil_qk_g=even_dil_qk_g, even_w_out=even_w_out,
             odd_w_in=odd_w_in, odd_qk_g=odd_qk_g, odd_sink=odd_sink, odd_w_out=odd_w_out,
             dense_w_up=dense_w_up, dense_w_down=dense_w_down, moe_router=moe_router,
             moe_w_up=moe_w_up, moe_w_down=moe_w_down)
    return _trunk(x_prompt, c_prompt, p), _trunk(x_sample, c_sample, p)
```

```python
import functools
import math

import jax
import jax.numpy as jnp
from jax import lax
from jax.experimental import pallas as pl
from jax.experimental.pallas import tpu as pltpu
from jax.experimental.pallas import tpu_sc as plsc

D_MODEL = 1024
DEPTH = 4
RMS_EPS = 1e-6
ROPE_THETA = 10000.0
NEG_INF = -1e30
HEAD_DIM = 64
MLA_HEADS = 8
MLA_NOPE = 64
MLA_ROPE = 32
MLA_QK = MLA_NOPE + MLA_ROPE
MLA_V = 64
MLA_Q_RANK = 384
MLA_KV_RANK = 256
DIL_HEADS = 8
DIL_PATTERNS = ((128, 1), (512, 4), (2048, 16))
WIN_Q_HEADS = 16
WIN_KV_HEADS = 4
WIN_HALF = 128
FF_DENSE = 2816
N_EXPERTS = 8
FF_EXPERT = 3584

LANES = 128
KR_OFF = MLA_Q_RANK + MLA_KV_RANK
DIL_OFF = KR_OFF + 2 * LANES
DIL_W = DIL_HEADS * HEAD_DIM

ROW_TILE = 512
MLA_Q_TILE = 1024
MLA_SUB_TILES = 4
BAND_Q_TILE = 128
BAND_UNROLL = 4
MOE_ROW_TILE = 1024
MOE_FF_CHUNK = 512
MXU_DIM = 256
DENSE_FF_CHUNKS = (6 * MXU_DIM, 5 * MXU_DIM)
assert sum(DENSE_FF_CHUNKS) == FF_DENSE
VMEM_LIMIT = 56 * 1024 * 1024

BF16 = jnp.bfloat16
F32 = jnp.float32


def _cparams(*sem):
    return pltpu.CompilerParams(dimension_semantics=sem, vmem_limit_bytes=VMEM_LIMIT)


def _const_spec(shape):
    nd = len(shape)
    return pl.BlockSpec(shape, lambda *_: (0,) * nd, pipeline_mode=pl.Buffered(1))


def _dot(a, b):
    return jnp.dot(a, b, preferred_element_type=F32)


def _dot_t(a, b):
    return lax.dot_general(a, b, (((1,), (1,)), ((), ())), preferred_element_type=F32)


def _split_bf16(a):
    hi = a.astype(BF16)
    lo = (a - hi.astype(F32)).astype(BF16)
    return hi, lo


def _dot3(a, b):
    ah, al = _split_bf16(a)
    bh, bl = _split_bf16(b)
    return _dot(ah, bh) + (_dot(ah, bl) + _dot(al, bh))


def _sigmoid(x):
    return 1.0 / (1.0 + jnp.exp(-x))


HALF_D = D_MODEL // 2
HI16 = 0xFFFF0000
PLANE_W = HALF_D // 2


def _pack_row(x):
    bits = lambda v: lax.bitcast_convert_type(v.astype(BF16).astype(F32), jnp.uint32)
    return (bits(x[:, :HALF_D]) >> 16) | (bits(x[:, HALF_D:]) & jnp.uint32(HI16))


def _unpack_row(w):
    return (lax.bitcast_convert_type(w << 16, F32),
            lax.bitcast_convert_type(w & jnp.uint32(HI16), F32))


def _norm_mod(x, g, sc, sh):
    ms = jnp.mean(x * x, axis=-1, keepdims=True)
    return (x * lax.rsqrt(ms + RMS_EPS) * g) * (1.0 + sc) + sh


def _rms(x, g):
    ms = jnp.mean(x * x, axis=-1, keepdims=True)
    return x * lax.rsqrt(ms + RMS_EPS) * g


def _lane_lo(shape=(1, LANES)):
    return lax.broadcasted_iota(jnp.int32, shape, len(shape) - 1) < HEAD_DIM


def _rope_slab(slab, partner, tab):
    return slab * tab[0] + partner * tab[1]


def _norm_rope_pair(slab, partner, tab):
    lo = _lane_lo()
    sq = slab * slab
    ss_all = jnp.sum(sq, axis=-1, keepdims=True)
    ss_lo = jnp.sum(jnp.where(lo, sq, 0.0), axis=-1, keepdims=True)
    r_lo = lax.rsqrt(ss_lo * (1.0 / HEAD_DIM) + RMS_EPS)
    r_hi = lax.rsqrt((ss_all - ss_lo) * (1.0 / HEAD_DIM) + RMS_EPS)
    return _rope_slab(slab, partner, tab) * jnp.where(lo, r_lo, r_hi)


def _ada_kernel(c_ref, w_ref, b_ref, o_ref):
    c = c_ref[...]
    o_ref[0] = _dot3(c * _sigmoid(c), w_ref[0]) + b_ref[0]


def _ada_mod(c, ada_w, ada_b):
    bc = c.shape[0]
    nchunk = ada_w.shape[2] // D_MODEL
    out = pl.pallas_call(
        _ada_kernel,
        grid=(DEPTH, nchunk),
        in_specs=[
            pl.BlockSpec((bc, D_MODEL), lambda l, j: (0, 0)),
            pl.BlockSpec((1, D_MODEL, D_MODEL), lambda l, j: (l, 0, j)),
            pl.BlockSpec((1, 1, D_MODEL), lambda l, j: (l, 0, j)),
        ],
        out_specs=pl.BlockSpec((1, bc, D_MODEL), lambda l, j: (l, 0, j)),
        out_shape=jax.ShapeDtypeStruct((DEPTH, bc, nchunk * D_MODEL), F32),
        compiler_params=_cparams("parallel", "parallel"),
        name="ada_mod",
    )(c, ada_w, ada_b.reshape(DEPTH, 1, -1))
    return out.reshape(DEPTH, bc, nchunk, D_MODEL)


def _even_in_kernel(x_ref, mod_ref, ng_ref, w_in_ref, qlg_ref, q_up_ref, kvlg_ref, kv_up_ref,
                    gkn_ref, tq_ref, tk_ref, tqb_ref, tkb_ref,
                    qa_ref, ka_ref, va_ref, qb_ref, kb_ref, vb_ref):
    h = _norm_mod(x_ref[0], ng_ref[...], mod_ref[0, 1:2, :], mod_ref[0, 0:1, :]).astype(BF16)
    proj = _dot(h, w_in_ref[...])

    cq = _rms(proj[:, :MLA_Q_RANK], qlg_ref[...]).astype(BF16)
    qraw = _dot(cq, q_up_ref[...])
    tq = (tq_ref[0], tq_ref[1])
    for hd in range(MLA_HEADS):
        slab = qraw[:, hd * LANES:(hd + 1) * LANES]
        partner = qraw[:, (MLA_HEADS + hd) * LANES:(MLA_HEADS + hd + 1) * LANES]
        ss = jnp.sum(slab * slab, axis=-1, keepdims=True)
        rinv = lax.rsqrt(ss * (1.0 / MLA_QK) + RMS_EPS)
        qa_ref[0, :, hd * LANES:(hd + 1) * LANES] = (
            _rope_slab(slab, partner, tq) * rinv).astype(BF16)

    ckv = _rms(proj[:, MLA_Q_RANK:KR_OFF], kvlg_ref[...]).astype(BF16)
    kvraw = _dot(ckv, kv_up_ref[...])
    kr = proj[:, KR_OFF:KR_OFF + LANES]
    ss_kr = jnp.sum(kr * kr, axis=-1, keepdims=True)
    kr_rot = _rope_slab(kr, proj[:, KR_OFF + LANES:DIL_OFF], (tk_ref[0], tk_ref[1]))
    gkn = gkn_ref[...]
    for hd in range(MLA_HEADS):
        slab = kvraw[:, hd * LANES:(hd + 1) * LANES]
        ss = jnp.sum(slab * slab, axis=-1, keepdims=True) + ss_kr
        rinv = lax.rsqrt(ss * (1.0 / MLA_QK) + RMS_EPS)
        ka_ref[0, :, hd * LANES:(hd + 1) * LANES] = ((slab * gkn + kr_rot) * rinv).astype(BF16)
    va_ref[0] = kvraw[:, MLA_HEADS * LANES:].astype(BF16)

    tqb = (tqb_ref[0], tqb_ref[1])
    tkb = (tkb_ref[0], tkb_ref[1])
    part_off = DIL_OFF + 3 * DIL_W
    for j in range(DIL_W // LANES):
        cq0, ck0 = j * LANES, DIL_W + j * LANES
        qb_ref[0, :, j * LANES:(j + 1) * LANES] = _norm_rope_pair(
            proj[:, DIL_OFF + cq0:DIL_OFF + cq0 + LANES],
            proj[:, part_off + cq0:part_off + cq0 + LANES], tqb).astype(BF16)
        kb_ref[0, :, j * LANES:(j + 1) * LANES] = _norm_rope_pair(
            proj[:, DIL_OFF + ck0:DIL_OFF + ck0 + LANES],
            proj[:, part_off + ck0:part_off + ck0 + LANES], tkb).astype(BF16)
    vb_ref[0] = proj[:, DIL_OFF + 2 * DIL_W:DIL_OFF + 3 * DIL_W].astype(BF16)


def _even_in(x, mod_l, ng, wts, tabs):
    b, t, _ = x.shape
    tm = min(ROW_TILE, t)
    w_in, qlg, q_up, kvlg, kv_up, gkn = wts
    row = lambda c: pl.BlockSpec((1, tm, c), lambda i, j: (i, j, 0))
    tab = pl.BlockSpec((2, tm, LANES), lambda i, j: (0, j, 0))
    hc = DIL_HEADS * HEAD_DIM
    outs = [jax.ShapeDtypeStruct((b, t, c), BF16)
            for c in (MLA_HEADS * LANES, MLA_HEADS * LANES, MLA_HEADS * MLA_V, hc, hc, hc)]
    return pl.pallas_call(
        _even_in_kernel,
        grid=(b, t // tm),
        in_specs=[
            row(D_MODEL),
            pl.BlockSpec((1, 6, D_MODEL), lambda i, j: (i, 0, 0)),
            _const_spec(ng.shape), _const_spec(w_in.shape), _const_spec(qlg.shape),
            _const_spec(q_up.shape), _const_spec(kvlg.shape), _const_spec(kv_up.shape),
            _const_spec(gkn.shape), tab, tab, tab, tab,
        ],
        out_specs=[row(s.shape[2]) for s in outs],
        out_shape=outs,
        compiler_params=_cparams("parallel", "parallel"),
        name="even_in",
    )(x, mod_l, ng, w_in, qlg, q_up, kvlg, kv_up, gkn, *tabs)


def _mla_kernel(q_ref, k_ref, v_ref, o_ref, vaug_s):
    lo = _lane_lo()

    @pl.when(pl.program_id(2) == 0)
    def _():
        v = v_ref[0]
        one = jnp.ones((), BF16)
        vaug_s[0] = jnp.where(lo, v, one)
        vaug_s[1] = jnp.where(lo, one, v)

    sub = q_ref.shape[1] // MLA_SUB_TILES
    for si in range(MLA_SUB_TILES):
        rows = slice(si * sub, (si + 1) * sub)
        outs = []
        for hd in range(2):
            q = q_ref[0, rows, hd * LANES:(hd + 1) * LANES]
            k = k_ref[0, :, hd * LANES:(hd + 1) * LANES]
            s = _dot_t(q, k)
            m = jnp.max(s, axis=-1, keepdims=True)
            p = jnp.exp2((s - m).astype(BF16))
            outs.append(_dot(p, vaug_s[hd]))
        num = jnp.where(lo, outs[0], outs[1])
        den = jnp.where(lo, pltpu.roll(outs[0], HEAD_DIM, 1), pltpu.roll(outs[1], HEAD_DIM, 1))
        o_ref[0, rows, :] = (num * (1.0 / den)).astype(BF16)


def _mla_attention(q, k, v):
    b, t, _ = q.shape
    tq = min(MLA_Q_TILE, t)
    return pl.pallas_call(
        _mla_kernel,
        grid=(b, MLA_HEADS // 2, t // tq),
        in_specs=[
            pl.BlockSpec((1, tq, 2 * LANES), lambda i, c, j: (i, j, c)),
            pl.BlockSpec((1, t, 2 * LANES), lambda i, c, j: (i, 0, c)),
            pl.BlockSpec((1, t, LANES), lambda i, c, j: (i, 0, c)),
        ],
        out_specs=pl.BlockSpec((1, tq, LANES), lambda i, c, j: (i, j, c)),
        out_shape=jax.ShapeDtypeStruct((b, t, MLA_HEADS * MLA_V), BF16),
        scratch_shapes=[pltpu.VMEM((2, t, LANES), BF16)],
        compiler_params=_cparams("parallel", "parallel", "arbitrary"),
        name="mla_attn",
    )(q, k, v)


def _band_chain(qs_list, kw, vw, mask, tq):
    lo = _lane_lo()
    m_lo = jnp.where(lo, 1.0, 0.0).astype(BF16)
    m_hi = jnp.where(lo, 0.0, 1.0).astype(BF16)
    parts = []
    for qs in qs_list:
        parts += [qs * m_lo, qs * m_hi]
    s = mask(_dot_t(jnp.concatenate(parts, axis=0), kw))
    m = jnp.max(s, axis=-1, keepdims=True)
    p = jnp.exp2(s - m)
    l = jnp.sum(p, axis=-1, keepdims=True)
    o = _dot(p.astype(BF16), vw) * (1.0 / l)
    lse = (m + jnp.log2(l)) * math.log(2.0)
    res = []
    for qi in range(len(qs_list)):
        a, b = 2 * qi * tq, (2 * qi + 1) * tq
        res.append((jnp.where(lo, o[a:a + tq], o[b:b + tq]),
                    jnp.where(lo, lse[a:a + tq], lse[b:b + tq])))
    return res


def _band_rel(rows, tq, win, half):
    return (lax.rem(lax.broadcasted_iota(jnp.int32, (rows, win), 0), tq)
            - lax.broadcasted_iota(jnp.int32, (rows, win), 1)) + half


def _band_valid(relh, off, half):
    return lax.bitcast_convert_type(relh + off, jnp.uint32) <= jnp.uint32(2 * half)


def _win_kernel(q_ref, k_ref, v_ref, sink_ref, o_ref, *, seq, tq, win):
    half, nq = WIN_HALF, 2
    relh = _band_rel(2 * nq * tq, tq, win, half)

    def tile(ti, carry):
        q0 = pl.multiple_of(ti * tq, tq)
        ws = 0 if win == seq else pl.multiple_of(jnp.clip(q0 - half, 0, seq - win), HEAD_DIM)
        valid = _band_valid(relh, q0 - ws, half)
        qs = [q_ref[0, pl.ds(q0, tq), qi * LANES:(qi + 1) * LANES] for qi in range(nq)]
        res = _band_chain(qs, k_ref[0, pl.ds(ws, win), :], v_ref[0, pl.ds(ws, win), :],
                          lambda s: jnp.where(valid, s, NEG_INF), tq)
        for qi, (o_slab, lse_slab) in enumerate(res):
            keep = _sigmoid(lse_slab - sink_ref[:, qi * LANES:(qi + 1) * LANES])
            o_ref[0, pl.ds(q0, tq), qi * LANES:(qi + 1) * LANES] = (o_slab * keep).astype(BF16)
        return carry

    n_t = seq // tq
    lax.fori_loop(0, n_t, tile, 0, unroll=min(BAND_UNROLL, n_t))


def _win_attention(q, k, v, sink):
    b, seq, _ = q.shape
    tq = min(BAND_Q_TILE, seq)
    win = min(tq + 2 * WIN_HALF, seq)
    assert tq >= WIN_HALF or win == seq
    kern = functools.partial(_win_kernel, seq=seq, tq=tq, win=win)
    return pl.pallas_call(
        kern,
        grid=(b, WIN_KV_HEADS),
        in_specs=[
            pl.BlockSpec((1, seq, 2 * LANES), lambda i, c: (i, 0, c)),
            pl.BlockSpec((1, seq, LANES), lambda i, c: (i, 0, c)),
            pl.BlockSpec((1, seq, LANES), lambda i, c: (i, 0, c)),
            pl.BlockSpec((1, 2 * LANES), lambda i, c: (0, c)),
        ],
        out_specs=pl.BlockSpec((1, seq, 2 * LANES), lambda i, c: (i, 0, c)),
        out_shape=jax.ShapeDtypeStruct(q.shape, BF16),
        compiler_params=_cparams("parallel", "parallel"),
        name="win_attn",
    )(q, k, v, sink)


def _dil_kernel(q_ref, k_ref, v_ref, o_ref, qf, kf, vf, qd, kd, vd, acc, lse, of, *, seq):
    qf[...] = q_ref[0].astype(F32)
    kf[...] = k_ref[0].astype(F32)
    vf[...] = v_ref[0].astype(F32)
    for pi in (1, 2):
        d = DIL_PATTERNS[pi][1]
        ln = seq // d
        for r in range(d):
            for src, dst in ((qf, qd), (kf, kd), (vf, vd)):
                dst[pi - 1, r * ln:(r + 1) * ln, :] = src[pl.ds(r, ln, stride=d), :].astype(BF16)

    for pi, (window, d) in enumerate(DIL_PATTERNS):
        ln = seq // d
        half = window // (2 * d)
        tq = min(BAND_Q_TILE, ln)
        win = min(tq + 2 * half, ln)
        n_t = ln // tq
        align = min(tq, HEAD_DIM)
        rel = _band_rel(2 * tq, tq, win, half)
        if pi == 0:
            load = lambda ref, row, n: ref[0, pl.ds(row, n), :]
            srcs = (q_ref, k_ref, v_ref)
        else:
            load = lambda ref, row, n, pi=pi: ref[pi - 1, pl.ds(row, n), :]
            srcs = (qd, kd, vd)

        def body(i, carry, pi=pi, ln=ln, half=half, tq=tq, win=win, n_t=n_t, align=align,
                 rel=rel, load=load, srcs=srcs):
            r = i // n_t
            q0 = (i - r * n_t) * tq
            ws = 0 if win == ln else jnp.clip(q0 - half, 0, ln - win)
            valid = _band_valid(rel, q0 - ws, half)
            rq = pl.multiple_of(r * ln + q0, align)
            rk = pl.multiple_of(r * ln + ws, align)
            ((o_slab, lse_slab),) = _band_chain(
                [load(srcs[0], rq, tq)], load(srcs[1], rk, win), load(srcs[2], rk, win),
                lambda s: jnp.where(valid, s, NEG_INF), tq)
            acc[pi, pl.ds(rq, tq), :] = o_slab
            lse[pi, pl.ds(rq, tq), :] = lse_slab
            return carry

        lax.fori_loop(0, seq // tq, body, 0, unroll=min(16, seq // tq))

    d2, d3 = DIL_PATTERNS[1][1], DIL_PATTERNS[2][1]
    l2n, l3n = seq // d2, seq // d3
    for r in range(d3):
        s1 = pl.ds(r, l3n, stride=d3)
        s2 = pl.ds((r % d2) * l2n + r // d2, l3n, stride=d3 // d2)
        s3 = pl.ds(r * l3n, l3n)
        l1, l2, l3 = lse[0, s1, :], lse[1, s2, :], lse[2, s3, :]
        m = jnp.maximum(jnp.maximum(l1, l2), l3)
        e1, e2, e3 = jnp.exp(l1 - m), jnp.exp(l2 - m), jnp.exp(l3 - m)
        of[s1, :] = ((e1 * acc[0, s1, :] + e2 * acc[1, s2, :] + e3 * acc[2, s3, :])
                     * (1.0 / (e1 + e2 + e3)))
    o_ref[0] = of[...].astype(BF16)


def _dil_attention(q, k, v):
    b, seq, c = q.shape
    blk = pl.BlockSpec((1, seq, LANES), lambda i, j: (i, 0, j))
    f32s = pltpu.VMEM((seq, LANES), F32)
    res = pltpu.VMEM((2, seq, LANES), BF16)
    pat = pltpu.VMEM((len(DIL_PATTERNS), seq, LANES), F32)
    return pl.pallas_call(
        functools.partial(_dil_kernel, seq=seq),
        grid=(b, c // LANES),
        in_specs=[blk, blk, blk],
        out_specs=blk,
        out_shape=jax.ShapeDtypeStruct(q.shape, BF16),
        scratch_shapes=[f32s, f32s, f32s, res, res, res, pat, pat, f32s],
        compiler_params=_cparams("parallel", "parallel"),
        name="dil_attn",
    )(q, k, v)


def _even_out_kernel(x_ref, mod_ref, oa_ref, ob_ref, w_ref, y_ref):
    half = w_ref.shape[0] // 2
    mix = _dot(oa_ref[0], w_ref[:half, :]) + _dot(ob_ref[0], w_ref[half:, :])
    y_ref[0] = x_ref[0] + mod_ref[0, 2:3, :] * mix


def _even_out(x, mod_l, oa, ob, w_out):
    b, t, _ = x.shape
    tm = min(ROW_TILE, t)
    row = lambda c: pl.BlockSpec((1, tm, c), lambda i, j: (i, j, 0))
    return pl.pallas_call(
        _even_out_kernel,
        grid=(b, t // tm),
        in_specs=[row(D_MODEL), pl.BlockSpec((1, 6, D_MODEL), lambda i, j: (i, 0, 0)),
                  row(oa.shape[2]), row(ob.shape[2]), _const_spec(w_out.shape)],
        out_specs=row(D_MODEL),
        out_shape=jax.ShapeDtypeStruct(x.shape, F32),
        compiler_params=_cparams("parallel", "parallel"),
        name="even_out",
    )(x, mod_l, oa, ob, w_out)


def _dense_ffn_kernel(x_ref, mod_ref, ng_ref, wu_ref, wd_ref, y_ref):
    x = x_ref[0]
    h = _norm_mod(x, ng_ref[...], mod_ref[0, 4:5, :], mod_ref[0, 3:4, :]).astype(BF16)
    ff = wd_ref.shape[0]
    acc = None
    c0 = 0
    for cw in DENSE_FF_CHUNKS:
        g = _dot(h, wu_ref[:, c0:c0 + cw])
        u = _dot(h, wu_ref[:, ff + c0:ff + c0 + cw])
        a = (g * _sigmoid(g) * u).astype(BF16)
        d = _dot(a, wd_ref[c0:c0 + cw, :])
        acc = d if acc is None else acc + d
        c0 += cw
    y_ref[0] = x + mod_ref[0, 5:6, :] * acc


def _dense_ffn(x, mod_l, ng, w_up, w_down):
    b, t, _ = x.shape
    tm = min(ROW_TILE, t)
    row = pl.BlockSpec((1, tm, D_MODEL), lambda i, j: (i, j, 0))
    return pl.pallas_call(
        _dense_ffn_kernel,
        grid=(b, t // tm),
        in_specs=[row, pl.BlockSpec((1, 6, D_MODEL), lambda i, j: (i, 0, 0)),
                  _const_spec(ng.shape), _const_spec(w_up.shape), _const_spec(w_down.shape)],
        out_specs=row,
        out_shape=jax.ShapeDtypeStruct(x.shape, F32),
        compiler_params=_cparams("parallel", "parallel"),
        name="dense_ffn",
    )(x, mod_l, ng, w_up, w_down)


def _odd_in_kernel(x_ref, mod_ref, ng_ref, w_in_ref, tq_ref, tk_ref, q_ref, k_ref, v_ref):
    h = _norm_mod(x_ref[0], ng_ref[...], mod_ref[0, 1:2, :], mod_ref[0, 0:1, :]).astype(BF16)
    proj = _dot(h, w_in_ref[...])
    tq = (tq_ref[0], tq_ref[1])
    tk = (tk_ref[0], tk_ref[1])
    nq = WIN_Q_HEADS * HEAD_DIM // LANES
    part = (nq + 2 * WIN_KV_HEADS) * LANES
    for j in range(nq):
        q_ref[0, :, j * LANES:(j + 1) * LANES] = _norm_rope_pair(
            proj[:, j * LANES:(j + 1) * LANES],
            proj[:, part + j * LANES:part + (j + 1) * LANES], tq).astype(BF16)
    for j in range(WIN_KV_HEADS):
        c0 = (nq + j) * LANES
        k_ref[0, :, j * LANES:(j + 1) * LANES] = _norm_rope_pair(
            proj[:, c0:c0 + LANES], proj[:, part + c0:part + c0 + LANES], tk).astype(BF16)
    v_ref[0] = proj[:, (nq + WIN_KV_HEADS) * LANES:part].astype(BF16)


def _odd_in(x, mod_l, ng, w_in, tabs):
    b, t, _ = x.shape
    tm = min(ROW_TILE, t)
    row = lambda c: pl.BlockSpec((1, tm, c), lambda i, j: (i, j, 0))
    tab = pl.BlockSpec((2, tm, LANES), lambda i, j: (0, j, 0))
    widths = (WIN_Q_HEADS * HEAD_DIM, WIN_KV_HEADS * LANES, WIN_KV_HEADS * LANES)
    return pl.pallas_call(
        _odd_in_kernel,
        grid=(b, t // tm),
        in_specs=[row(D_MODEL), pl.BlockSpec((1, 6, D_MODEL), lambda i, j: (i, 0, 0)),
                  _const_spec(ng.shape), _const_spec(w_in.shape), tab, tab],
        out_specs=[row(c) for c in widths],
        out_shape=[jax.ShapeDtypeStruct((b, t, c), BF16) for c in widths],
        compiler_params=_cparams("parallel", "parallel"),
        name="odd_in",
    )(x, mod_l, ng, w_in, *tabs)


def _odd_out_kernel(x_ref, mod_ref, o_ref, w_ref, y_ref):
    y_ref[0] = x_ref[0] + mod_ref[0, 2:3, :] * _dot(o_ref[0], w_ref[...])


def _odd_out(x, mod_l, o, w_out):
    b, t, _ = x.shape
    tm = min(ROW_TILE, t)
    row = pl.BlockSpec((1, tm, D_MODEL), lambda i, j: (i, j, 0))
    return pl.pallas_call(
        _odd_out_kernel,
        grid=(b, t // tm),
        in_specs=[row, pl.BlockSpec((1, 6, D_MODEL), lambda i, j: (i, 0, 0)), row,
                  _const_spec(w_out.shape)],
        out_specs=row,
        out_shape=jax.ShapeDtypeStruct(x.shape, F32),
        compiler_params=_cparams("parallel", "parallel"),
        name="odd_out",
    )(x, mod_l, o, w_out)


def _router_kernel(x_ref, mod_ref, ng_ref, wr_ref, h_ref, gate_ref, sel_ref):
    h = _norm_mod(x_ref[0], ng_ref[...], mod_ref[0, 4:5, :], mod_ref[0, 3:4, :])
    w = _pack_row(h)
    h_ref[0, 0] = w[:, :PLANE_W]
    h_ref[1, 0] = w[:, PLANE_W:]
    lane = lax.broadcasted_iota(jnp.int32, (1, LANES), 1).astype(F32)
    logits = jnp.where(lane < N_EXPERTS, _dot3(h, wr_ref[...]), NEG_INF)
    e = jnp.exp(logits - jnp.max(logits, axis=-1, keepdims=True))
    p = e * (1.0 / jnp.sum(e, axis=-1, keepdims=True))
    p = jnp.where(lane < N_EXPERTS, p, -1.0)
    m1 = jnp.max(p, axis=-1, keepdims=True)
    i1 = jnp.min(jnp.where(p == m1, lane, float(LANES)), axis=-1, keepdims=True)
    s1 = lane == i1
    p2 = jnp.where(s1, -1.0, p)
    m2 = jnp.max(p2, axis=-1, keepdims=True)
    i2 = jnp.min(jnp.where(p2 == m2, lane, float(LANES)), axis=-1, keepdims=True)
    s2 = lane == i2
    inv = 1.0 / (m1 + m2)
    gate_ref[0] = jnp.where(s1, m1 * inv, 0.0) + jnp.where(s2, m2 * inv, 0.0)
    sel_ref[0] = jnp.where(s1, 1.0, 0.0) + jnp.where(s2, 1.0, 0.0)


def _router(x, mod_l, ng, wr):
    b, t, _ = x.shape
    tm = min(ROW_TILE, t)
    row = lambda c: pl.BlockSpec((1, tm, c), lambda i, j: (i, j, 0))
    return pl.pallas_call(
        _router_kernel,
        grid=(b, t // tm),
        in_specs=[row(D_MODEL), pl.BlockSpec((1, 6, D_MODEL), lambda i, j: (i, 0, 0)),
                  _const_spec(ng.shape), _const_spec(wr.shape)],
        out_specs=[pl.BlockSpec((2, 1, tm, PLANE_W), lambda i, j: (0, i, j, 0)), row(LANES), row(LANES)],
        out_shape=[jax.ShapeDtypeStruct((2, b, t, PLANE_W), jnp.uint32),
                   jax.ShapeDtypeStruct((b, t, LANES), F32),
                   jax.ShapeDtypeStruct((b, t, LANES), F32)],
        compiler_params=_cparams("parallel", "parallel"),
        name="moe_router",
    )(x, mod_l, ng, wr)


SC_WINDOW = 128
SC_COLS = PLANE_W


def _sc_mesh():
    return plsc.VectorSubcoreMesh(core_axis_name="core", subcore_axis_name="subcore")


def _scatter_rows_sc(src, pos_a, pos_b, n_out):
    n = src.shape[0]

    @pl.kernel(out_type=jax.ShapeDtypeStruct((n_out, SC_COLS), src.dtype), mesh=_sc_mesh(),
               scratch_types=[])
    def scatter(src_hbm, ia_hbm, ib_hbm, dst_hbm):
        def body(rows_vmem, ia_vmem, ib_vmem):
            pltpu.sync_copy(rows_vmem, dst_hbm.at[ia_vmem.at[0]])
            pltpu.sync_copy(rows_vmem, dst_hbm.at[ib_vmem.at[0]])

        idx_spec = pl.BlockSpec((1, SC_WINDOW), lambda i: (0, i))
        pltpu.emit_pipeline(
            body,
            grid=(n // SC_WINDOW,),
            in_specs=[pl.BlockSpec((SC_WINDOW, SC_COLS), lambda i: (i, 0)), idx_spec, idx_spec],
            out_specs=[],
            core_axis_name=("core", "subcore"),
            dimension_semantics=(pltpu.PARALLEL,),
        )(src_hbm, ia_hbm, ib_hbm)

    return scatter(src, pos_a.reshape(1, n), pos_b.reshape(1, n))


def _gather_rows_sc(src, pos):
    n = pos.shape[0]

    @pl.kernel(out_type=jax.ShapeDtypeStruct((n, SC_COLS), src.dtype), mesh=_sc_mesh(),
               scratch_types=[])
    def gather(src_hbm, i_hbm, dst_hbm):
        def body(i_vmem, rows_vmem):
            pltpu.sync_copy(src_hbm.at[i_vmem.at[0]], rows_vmem)

        pltpu.emit_pipeline(
            body,
            grid=(n // SC_WINDOW,),
            in_specs=[pl.BlockSpec((1, SC_WINDOW), lambda i: (0, i))],
            out_specs=[pl.BlockSpec((SC_WINDOW, SC_COLS), lambda i: (i, 0))],
            core_axis_name=("core", "subcore"),
            dimension_semantics=(pltpu.PARALLEL,),
        )(i_hbm, dst_hbm)

    return gather(src, pos.reshape(1, n))


def _expert_kernel(te_ref, nu_ref, x0_ref, x1_ref, wg_ref, wu_ref, wd_ref, ys_ref, xb_ref, acc_ref):
    i, j = pl.program_id(0), pl.program_id(1)
    used = i < nu_ref[0]

    @pl.when(jnp.logical_and(used, j == 0))
    def _():
        for pi, x_ref in enumerate((x0_ref, x1_ref)):
            lo, hi = _unpack_row(x_ref[0])
            xb_ref[:, pi * PLANE_W:(pi + 1) * PLANE_W] = lo.astype(BF16)
            xb_ref[:, HALF_D + pi * PLANE_W:HALF_D + (pi + 1) * PLANE_W] = hi.astype(BF16)
        acc_ref[...] = jnp.zeros_like(acc_ref)

    @pl.when(used)
    def _():
        xb = xb_ref[...]
        g = _dot(xb, wg_ref[0])
        u = _dot(xb, wu_ref[0])
        a = (g * _sigmoid(g) * u).astype(BF16)
        acc_ref[...] += _dot(a, wd_ref[0])

    last = j == pl.num_programs(1) - 1

    @pl.when(jnp.logical_and(used, last))
    def _():
        w = _pack_row(acc_ref[...])
        ys_ref[0] = w[:, :PLANE_W]
        ys_ref[1] = w[:, PLANE_W:]

    @pl.when(jnp.logical_and(jnp.logical_not(used), last))
    def _():
        ys_ref[...] = jnp.zeros_like(ys_ref)


def _experts(xs, tile_expert, n_used, w_up, w_down):
    p = xs.shape[1]
    nck = FF_EXPERT // MOE_FF_CHUNK
    tm = MOE_ROW_TILE

    def chunk(i, j, te, nu):
        return jnp.where(i < nu[0], j, nck - 1)

    grid_spec = pltpu.PrefetchScalarGridSpec(
        num_scalar_prefetch=2,
        grid=(p // tm, nck),
        in_specs=[
            pl.BlockSpec((1, tm, PLANE_W), lambda i, j, te, nu: (0, i, 0)),
            pl.BlockSpec((1, tm, PLANE_W), lambda i, j, te, nu: (1, i, 0)),
            pl.BlockSpec((1, D_MODEL, MOE_FF_CHUNK), lambda i, j, te, nu: (te[i], 0, chunk(i, j, te, nu))),
            pl.BlockSpec((1, D_MODEL, MOE_FF_CHUNK),
                         lambda i, j, te, nu: (te[i], 0, nck + chunk(i, j, te, nu))),
            pl.BlockSpec((1, MOE_FF_CHUNK, D_MODEL), lambda i, j, te, nu: (te[i], chunk(i, j, te, nu), 0)),
        ],
        out_specs=pl.BlockSpec((2, tm, PLANE_W), lambda i, j, te, nu: (0, i, 0)),
        scratch_shapes=[pltpu.VMEM((tm, D_MODEL), BF16), pltpu.VMEM((tm, D_MODEL), F32)],
    )
    return pl.pallas_call(
        _expert_kernel,
        grid_spec=grid_spec,
        out_shape=jax.ShapeDtypeStruct((2, p, PLANE_W), jnp.uint32),
        compiler_params=_cparams("arbitrary", "arbitrary"),
        name="moe_experts",
    )(tile_expert, n_used, xs, xs, w_up, w_up, w_down)


def _combine_kernel(x_ref, mod_ref, g_ref, a0_ref, b0_ref, a1_ref, b1_ref, o_ref):
    g = g_ref[0]
    ga, gb = g[:, 0:1], g[:, 1:2]
    for pi, (a_ref, b_ref) in enumerate(((a0_ref, b0_ref), (a1_ref, b1_ref))):
        a_lo, a_hi = _unpack_row(a_ref[0, 0, 0])
        b_lo, b_hi = _unpack_row(b_ref[0, 0, 0])
        for c0, ff in ((pi * PLANE_W, ga * a_lo + gb * b_lo),
                       (HALF_D + pi * PLANE_W, ga * a_hi + gb * b_hi)):
            cols = slice(c0, c0 + PLANE_W)
            o_ref[0, :, cols] = x_ref[0, :, cols] + mod_ref[0, 5:6, cols] * ff


def _combine(x, mod_l, gate_ab, yg):
    b, t, _ = x.shape
    tm = min(ROW_TILE, t)
    row = pl.BlockSpec((1, tm, D_MODEL), lambda i, j: (i, j, 0))
    part = lambda pi, ab: pl.BlockSpec((1, 1, 1, tm, PLANE_W), lambda i, j: (pi, ab, i, j, 0))
    return pl.pallas_call(
        _combine_kernel,
        grid=(b, t // tm),
        in_specs=[row, pl.BlockSpec((1, 6, D_MODEL), lambda i, j: (i, 0, 0)),
                  pl.BlockSpec((1, tm, 2), lambda i, j: (i, j, 0)),
                  part(0, 0), part(0, 1), part(1, 0), part(1, 1)],
        out_specs=row,
        out_shape=jax.ShapeDtypeStruct(x.shape, F32),
        compiler_params=_cparams("parallel", "parallel"),
        name="moe_combine",
    )(x, mod_l, gate_ab, yg, yg, yg, yg)


def _moe(x, mod_l, ng, wr, w_up, w_down):
    b, t, _ = x.shape
    n = b * t
    h, gates, sel = _router(x, mod_l, ng, wr)
    gates = gates.reshape(n, LANES)[:, :N_EXPERTS]
    sel = sel.reshape(n, LANES)[:, :N_EXPERTS] > 0.5

    tm = MOE_ROW_TILE
    n_tiles = 2 * n // tm + N_EXPERTS
    p = n_tiles * tm
    sel_i = sel.astype(jnp.int32)
    rank = jnp.cumsum(sel_i, axis=0) - sel_i
    counts = jnp.sum(sel_i, axis=0)
    padded = (counts + tm - 1) // tm * tm
    ends = jnp.cumsum(padded)
    pos = (ends - padded)[None, :] + rank
    tile_start = jnp.arange(n_tiles, dtype=jnp.int32) * tm
    tile_expert = jnp.minimum(jnp.sum((tile_start[:, None] >= ends[None, :]).astype(jnp.int32), axis=1),
                              N_EXPERTS - 1)
    n_used = (ends[-1] // tm).astype(jnp.int32).reshape(1)
    order = jnp.cumsum(sel_i, axis=1)
    first, second = sel & (order == 1), sel & (order == 2)
    pos_a = jnp.sum(jnp.where(first, pos, 0), axis=1)
    pos_b = jnp.sum(jnp.where(second, pos, 0), axis=1)
    gate_ab = jnp.stack([jnp.sum(jnp.where(first, gates, 0.0), axis=1),
                         jnp.sum(jnp.where(second, gates, 0.0), axis=1)], axis=1)

    xs = _scatter_rows_sc(h.reshape(2 * n, PLANE_W), jnp.concatenate([pos_a, pos_a + p]),
                          jnp.concatenate([pos_b, pos_b + p]), 2 * p)
    ys = _experts(xs.reshape(2, p, PLANE_W), tile_expert, n_used, w_up, w_down)
    yg = _gather_rows_sc(ys.reshape(2 * p, PLANE_W),
                         jnp.concatenate([pos_a, pos_b, pos_a + p, pos_b + p]))
    return _combine(x, mod_l, gate_ab.reshape(b, t, 2), yg.reshape(2, 2, b, t, PLANE_W))


def _rope_tables(t, dim):
    inv = ROPE_THETA ** (-jnp.arange(0, dim, 2, dtype=F32) / dim)
    ang = jnp.arange(t, dtype=F32)[:, None] * inv[None, :]
    return jnp.cos(ang), jnp.sin(ang)


def _rope_lane_tables(cos, sin, gain, offs, scale):
    t, hw = cos.shape
    c, s = jnp.ones((t, LANES), F32), jnp.zeros((t, LANES), F32)
    g_up, g_dn = jnp.roll(gain, hw), jnp.roll(gain, -hw)
    g_partner = jnp.zeros((LANES,), F32)
    for o in offs:
        c = c.at[:, o:o + hw].set(cos).at[:, o + hw:o + 2 * hw].set(cos)
        s = s.at[:, o:o + hw].set(-sin).at[:, o + hw:o + 2 * hw].set(sin)
        g_partner = g_partner.at[o:o + hw].set(g_dn[o:o + hw]).at[o + hw:o + 2 * hw].set(
            g_up[o + hw:o + 2 * hw])
    return jnp.stack([c * gain, s * g_partner]) * scale


def _swap_halves(w, width):
    d, c = w.shape
    return w.reshape(d, c // width, 2, width // 2)[:, :, ::-1, :].reshape(d, c)


def _pad_lanes(v, width=LANES):
    return jnp.zeros((width,), F32).at[:v.shape[0]].set(v)


def _even_weights(w_in, qlg, q_up, kvlg, kv_up):
    w_kr = w_in[:, KR_OFF:KR_OFF + MLA_ROPE]
    slab = lambda w: jnp.zeros((D_MODEL, LANES), F32).at[:, MLA_NOPE:MLA_QK].set(w)
    w_dil = w_in[:, KR_OFF + MLA_ROPE:]
    n_qk = 2 * DIL_HEADS * HEAD_DIM
    w_in_p = jnp.concatenate([w_in[:, :KR_OFF], slab(w_kr), slab(_swap_halves(w_kr, MLA_ROPE)),
                              w_dil, _swap_halves(w_dil[:, :n_qk], HEAD_DIM)], axis=1)
    q3 = q_up.reshape(MLA_Q_RANK, MLA_HEADS, MLA_QK)
    q_rope_p = _swap_halves(q3[:, :, MLA_NOPE:].reshape(MLA_Q_RANK, -1), MLA_ROPE)
    q_part = jnp.pad(q_rope_p.reshape(MLA_Q_RANK, MLA_HEADS, MLA_ROPE),
                     ((0, 0), (0, 0), (MLA_NOPE, LANES - MLA_QK)))
    q_up_p = jnp.concatenate([jnp.pad(q3, ((0, 0), (0, 0), (0, LANES - MLA_QK))), q_part],
                             axis=1).reshape(MLA_Q_RANK, 2 * MLA_HEADS * LANES)
    kv3 = kv_up.reshape(MLA_KV_RANK, MLA_HEADS, MLA_NOPE + MLA_V)
    k_p = jnp.pad(kv3[:, :, :MLA_NOPE], ((0, 0), (0, 0), (0, LANES - MLA_NOPE)))
    kv_up_p = jnp.concatenate([k_p.reshape(MLA_KV_RANK, -1),
                               kv3[:, :, MLA_NOPE:].reshape(MLA_KV_RANK, -1)], axis=1)
    return (w_in_p.astype(BF16), qlg.reshape(1, -1), q_up_p.astype(BF16), kvlg.reshape(1, -1),
            kv_up_p.astype(BF16))


def _odd_weights(w_in):
    nq = WIN_Q_HEADS * HEAD_DIM
    nkv = WIN_KV_HEADS * HEAD_DIM
    dup = lambda w: jnp.concatenate([w.reshape(D_MODEL, WIN_KV_HEADS, 1, HEAD_DIM)] * 2,
                                    axis=2).reshape(D_MODEL, 2 * nkv)
    w_q, w_k = w_in[:, :nq], dup(w_in[:, nq:nq + nkv])
    return jnp.concatenate([w_q, w_k, dup(w_in[:, nq + nkv:]), _swap_halves(w_q, HEAD_DIM),
                            _swap_halves(w_k, HEAD_DIM)], axis=1).astype(BF16)


def _trunk(x, c, p):
    b, t, _ = x.shape
    mod = _ada_mod(c, p["ada_w"], p["ada_b"])
    cos_r, sin_r = _rope_tables(t, MLA_ROPE)
    cos_h, sin_h = _rope_tables(t, HEAD_DIM)
    pair_offs = (0, HEAD_DIM)
    for l in range(DEPTH):
        i = l // 2
        mod_l = mod[l]
        ng = p["norm_g"][l]
        if l % 2 == 0:
            g_q, g_k = p["even_mla_qk_g"][i, 0], p["even_mla_qk_g"][i, 1]
            g_qb, g_kb = p["even_dil_qk_g"][i, 0], p["even_dil_qk_g"][i, 1]
            tabs = (
                _rope_lane_tables(cos_r, sin_r, _pad_lanes(g_q), (MLA_NOPE,),
                                  MLA_QK ** -0.5 * math.log2(math.e)),
                _rope_lane_tables(cos_r, sin_r, _pad_lanes(g_k).at[:MLA_NOPE].set(0.0), (MLA_NOPE,), 1.0),
                _rope_lane_tables(cos_h, sin_h, jnp.tile(g_qb, 2), pair_offs,
                                  HEAD_DIM ** -0.5 * math.log2(math.e)),
                _rope_lane_tables(cos_h, sin_h, jnp.tile(g_kb, 2), pair_offs, 1.0),
            )
            gkn = _pad_lanes(g_k[:MLA_NOPE]).reshape(1, LANES)
            wts = _even_weights(p["even_w_in"][i], p["even_q_latent_g"][i], p["even_q_up"][i],
                                p["even_kv_latent_g"][i], p["even_kv_up"][i]) + (gkn,)
            qa, ka, va, qb, kb, vb = _even_in(x, mod_l, ng[0:1], wts, tabs)
            oa = _mla_attention(qa, ka, va)
            ob = _dil_attention(qb, kb, vb)
            x = _even_out(x, mod_l, oa, ob, p["even_w_out"][i].astype(BF16))
            x = _dense_ffn(x, mod_l, ng[1:2], p["dense_w_up"][i].astype(BF16),
                           p["dense_w_down"][i].astype(BF16))
        else:
            g_q, g_k = p["odd_qk_g"][i, 0], p["odd_qk_g"][i, 1]
            tabs = (
                _rope_lane_tables(cos_h, sin_h, jnp.tile(g_q, 2), pair_offs,
                                  HEAD_DIM ** -0.5 * math.log2(math.e)),
                _rope_lane_tables(cos_h, sin_h, jnp.tile(g_k, 2), pair_offs, 1.0),
            )
            q, k, v = _odd_in(x, mod_l, ng[0:1], _odd_weights(p["odd_w_in"][i]), tabs)
            sink = jnp.repeat(p["odd_sink"][i], HEAD_DIM).reshape(1, -1)
            o = _win_attention(q, k, v, sink)
            x = _odd_out(x, mod_l, o, p["odd_w_out"][i].astype(BF16))
            wr = jnp.zeros((D_MODEL, LANES), F32).at[:, :N_EXPERTS].set(p["moe_router"][i])
            x = _moe(x, mod_l, ng[1:2], wr, p["moe_w_up"][i].astype(BF16),
                     p["moe_w_down"][i].astype(BF16))
    return x


def kernel(x_prompt, x_sample, c_prompt, c_sample, ada_w, ada_b, norm_g, even_w_in, even_q_latent_g, even_q_up, even_kv_latent_g, even_kv_up, even_mla_qk_g, even_dil_qk_g, even_w_out, odd_w_in, odd_qk_g, odd_sink, odd_w_out, dense_w_up, dense_w_down, moe_router, moe_w_up, moe_w_down):
    p = dict(ada_w=ada_w, ada_b=ada_b, norm_g=norm_g, even_w_in=even_w_in,
             even_q_latent_g=even_q_latent_g, even_q_up=even_q_up,
             even_kv_latent_g=even_kv_latent_g, even_kv_up=even_kv_up,
             even_mla_qk_g=even_mla_qk_g, even_dil_qk_g=even_dil_qk_g, even_w_out=even_w_out,
             odd_w_in=odd_w_in, odd_qk_g=odd_qk_g, odd_sink=odd_sink, odd_w_out=odd_w_out,
             dense_w_up=dense_w_up, dense_w_down=dense_w_down, moe_router=moe_router,
             moe_w_up=moe_w_up, moe_w_down=moe_w_down)
    return _trunk(x_prompt, c_prompt, p), _trunk(x_sample, c_sample, p)
```

```python
import functools
import math

import jax
import jax.numpy as jnp
from jax import lax
from jax.experimental import pallas as pl
from jax.experimental.pallas import tpu as pltpu
from jax.experimental.pallas import tpu_sc as plsc

D_MODEL = 1024
DEPTH = 4
RMS_EPS = 1e-6
ROPE_THETA = 10000.0
NEG_INF = -1e30
HEAD_DIM = 64
MLA_HEADS = 8
MLA_NOPE = 64
MLA_ROPE = 32
MLA_QK = MLA_NOPE + MLA_ROPE
MLA_V = 64
MLA_Q_RANK = 384
MLA_KV_RANK = 256
DIL_HEADS = 8
DIL_PATTERNS = ((128, 1), (512, 4), (2048, 16))
WIN_Q_HEADS = 16
WIN_KV_HEADS = 4
WIN_HALF = 128
FF_DENSE = 2816
N_EXPERTS = 8
FF_EXPERT = 3584

LANES = 128
KR_OFF = MLA_Q_RANK + MLA_KV_RANK
DIL_OFF = KR_OFF + 2 * LANES
DIL_W = DIL_HEADS * HEAD_DIM

ROW_TILE = 512
STREAM_ROW_TILE = 1024
MLA_Q_TILE = 1024
MLA_SUB_TILES = 4
BAND_Q_TILE = 128
BAND_UNROLL = 4
MOE_ROW_TILE = 1024
MOE_FF_CHUNK = 512
MXU_DIM = 256
DENSE_FF_CHUNKS = (6 * MXU_DIM, 5 * MXU_DIM)
assert sum(DENSE_FF_CHUNKS) == FF_DENSE
VMEM_LIMIT = 56 * 1024 * 1024

BF16 = jnp.bfloat16
F32 = jnp.float32


def _cparams(*sem):
    return pltpu.CompilerParams(dimension_semantics=sem, vmem_limit_bytes=VMEM_LIMIT)


def _const_spec(shape):
    nd = len(shape)
    return pl.BlockSpec(shape, lambda *_: (0,) * nd, pipeline_mode=pl.Buffered(1))


def _dot(a, b):
    return jnp.dot(a, b, preferred_element_type=F32)


def _dot_t(a, b):
    return lax.dot_general(a, b, (((1,), (1,)), ((), ())), preferred_element_type=F32)


def _split_bf16(a):
    hi = a.astype(BF16)
    lo = (a - hi.astype(F32)).astype(BF16)
    return hi, lo


def _dot3(a, b):
    ah, al = _split_bf16(a)
    bh, bl = _split_bf16(b)
    return _dot(ah, bh) + (_dot(ah, bl) + _dot(al, bh))


def _sigmoid(x):
    return 1.0 / (1.0 + jnp.exp(-x))


HALF_D = D_MODEL // 2
HI16 = 0xFFFF0000
PLANE_W = HALF_D // 2


def _pack_row(x):
    bits = lambda v: lax.bitcast_convert_type(v.astype(BF16).astype(F32), jnp.uint32)
    return (bits(x[:, :HALF_D]) >> 16) | (bits(x[:, HALF_D:]) & jnp.uint32(HI16))


def _unpack_row(w):
    return (lax.bitcast_convert_type(w << 16, F32),
            lax.bitcast_convert_type(w & jnp.uint32(HI16), F32))


def _norm_mod(x, g, sc, sh):
    ms = jnp.mean(x * x, axis=-1, keepdims=True)
    return (x * lax.rsqrt(ms + RMS_EPS) * g) * (1.0 + sc) + sh


def _rms(x, g):
    ms = jnp.mean(x * x, axis=-1, keepdims=True)
    return x * lax.rsqrt(ms + RMS_EPS) * g


def _lane_lo(shape=(1, LANES)):
    return lax.broadcasted_iota(jnp.int32, shape, len(shape) - 1) < HEAD_DIM


def _rope_slab(slab, partner, tab):
    return slab * tab[0] + partner * tab[1]


def _norm_rope_pair(slab, partner, tab):
    lo = _lane_lo()
    sq = slab * slab
    ss_all = jnp.sum(sq, axis=-1, keepdims=True)
    ss_lo = jnp.sum(jnp.where(lo, sq, 0.0), axis=-1, keepdims=True)
    r_lo = lax.rsqrt(ss_lo * (1.0 / HEAD_DIM) + RMS_EPS)
    r_hi = lax.rsqrt((ss_all - ss_lo) * (1.0 / HEAD_DIM) + RMS_EPS)
    return _rope_slab(slab, partner, tab) * jnp.where(lo, r_lo, r_hi)


def _ada_kernel(c_ref, w_ref, b_ref, o_ref):
    c = c_ref[...]
    o_ref[0] = _dot3(c * _sigmoid(c), w_ref[0]) + b_ref[0]


def _ada_mod(c, ada_w, ada_b):
    bc = c.shape[0]
    nchunk = ada_w.shape[2] // D_MODEL
    out = pl.pallas_call(
        _ada_kernel,
        grid=(DEPTH, nchunk),
        in_specs=[
            pl.BlockSpec((bc, D_MODEL), lambda l, j: (0, 0)),
            pl.BlockSpec((1, D_MODEL, D_MODEL), lambda l, j: (l, 0, j)),
            pl.BlockSpec((1, 1, D_MODEL), lambda l, j: (l, 0, j)),
        ],
        out_specs=pl.BlockSpec((1, bc, D_MODEL), lambda l, j: (l, 0, j)),
        out_shape=jax.ShapeDtypeStruct((DEPTH, bc, nchunk * D_MODEL), F32),
        compiler_params=_cparams("parallel", "parallel"),
        name="ada_mod",
    )(c, ada_w, ada_b.reshape(DEPTH, 1, -1))
    return out.reshape(DEPTH, bc, nchunk, D_MODEL)


def _even_in_kernel(x_ref, mod_ref, ng_ref, w_in_ref, qlg_ref, q_up_ref, kvlg_ref, kv_up_ref,
                    gkn_ref, tq_ref, tk_ref, tqb_ref, tkb_ref,
                    qa_ref, ka_ref, va_ref, qb_ref, kb_ref, vb_ref):
    h = _norm_mod(x_ref[0], ng_ref[...], mod_ref[0, 1:2, :], mod_ref[0, 0:1, :]).astype(BF16)
    proj = _dot(h, w_in_ref[...])

    cq = _rms(proj[:, :MLA_Q_RANK], qlg_ref[...]).astype(BF16)
    qraw = _dot(cq, q_up_ref[...])
    tq = (tq_ref[0], tq_ref[1])
    for hd in range(MLA_HEADS):
        slab = qraw[:, hd * LANES:(hd + 1) * LANES]
        partner = qraw[:, (MLA_HEADS + hd) * LANES:(MLA_HEADS + hd + 1) * LANES]
        ss = jnp.sum(slab * slab, axis=-1, keepdims=True)
        rinv = lax.rsqrt(ss * (1.0 / MLA_QK) + RMS_EPS)
        qa_ref[0, :, hd * LANES:(hd + 1) * LANES] = (
            _rope_slab(slab, partner, tq) * rinv).astype(BF16)

    ckv = _rms(proj[:, MLA_Q_RANK:KR_OFF], kvlg_ref[...]).astype(BF16)
    kvraw = _dot(ckv, kv_up_ref[...])
    kr = proj[:, KR_OFF:KR_OFF + LANES]
    ss_kr = jnp.sum(kr * kr, axis=-1, keepdims=True)
    kr_rot = _rope_slab(kr, proj[:, KR_OFF + LANES:DIL_OFF], (tk_ref[0], tk_ref[1]))
    gkn = gkn_ref[...]
    for hd in range(MLA_HEADS):
        slab = kvraw[:, hd * LANES:(hd + 1) * LANES]
        ss = jnp.sum(slab * slab, axis=-1, keepdims=True) + ss_kr
        rinv = lax.rsqrt(ss * (1.0 / MLA_QK) + RMS_EPS)
        ka_ref[0, :, hd * LANES:(hd + 1) * LANES] = ((slab * gkn + kr_rot) * rinv).astype(BF16)
    va_ref[0] = kvraw[:, MLA_HEADS * LANES:].astype(BF16)

    tqb = (tqb_ref[0], tqb_ref[1])
    tkb = (tkb_ref[0], tkb_ref[1])
    part_off = DIL_OFF + 3 * DIL_W
    for j in range(DIL_W // LANES):
        cq0, ck0 = j * LANES, DIL_W + j * LANES
        qb_ref[0, :, j * LANES:(j + 1) * LANES] = _norm_rope_pair(
            proj[:, DIL_OFF + cq0:DIL_OFF + cq0 + LANES],
            proj[:, part_off + cq0:part_off + cq0 + LANES], tqb).astype(BF16)
        kb_ref[0, :, j * LANES:(j + 1) * LANES] = _norm_rope_pair(
            proj[:, DIL_OFF + ck0:DIL_OFF + ck0 + LANES],
            proj[:, part_off + ck0:part_off + ck0 + LANES], tkb).astype(BF16)
    vb_ref[0] = proj[:, DIL_OFF + 2 * DIL_W:DIL_OFF + 3 * DIL_W].astype(BF16)


def _even_in(x, mod_l, ng, wts, tabs):
    b, t, _ = x.shape
    tm = min(ROW_TILE, t)
    w_in, qlg, q_up, kvlg, kv_up, gkn = wts
    row = lambda c: pl.BlockSpec((1, tm, c), lambda i, j: (i, j, 0))
    tab = pl.BlockSpec((2, tm, LANES), lambda i, j: (0, j, 0))
    hc = DIL_HEADS * HEAD_DIM
    outs = [jax.ShapeDtypeStruct((b, t, c), BF16)
            for c in (MLA_HEADS * LANES, MLA_HEADS * LANES, MLA_HEADS * MLA_V, hc, hc, hc)]
    return pl.pallas_call(
        _even_in_kernel,
        grid=(b, t // tm),
        in_specs=[
            row(D_MODEL),
            pl.BlockSpec((1, 6, D_MODEL), lambda i, j: (i, 0, 0)),
            _const_spec(ng.shape), _const_spec(w_in.shape), _const_spec(qlg.shape),
            _const_spec(q_up.shape), _const_spec(kvlg.shape), _const_spec(kv_up.shape),
            _const_spec(gkn.shape), tab, tab, tab, tab,
        ],
        out_specs=[row(s.shape[2]) for s in outs],
        out_shape=outs,
        compiler_params=_cparams("parallel", "parallel"),
        name="even_in",
    )(x, mod_l, ng, w_in, qlg, q_up, kvlg, kv_up, gkn, *tabs)


def _mla_kernel(q_ref, k_ref, v_ref, o_ref, vaug_s):
    lo = _lane_lo()

    @pl.when(pl.program_id(2) == 0)
    def _():
        v = v_ref[0]
        one = jnp.ones((), BF16)
        vaug_s[0] = jnp.where(lo, v, one)
        vaug_s[1] = jnp.where(lo, one, v)

    sub = q_ref.shape[1] // MLA_SUB_TILES
    for si in range(MLA_SUB_TILES):
        rows = slice(si * sub, (si + 1) * sub)
        outs = []
        for hd in range(2):
            q = q_ref[0, rows, hd * LANES:(hd + 1) * LANES]
            k = k_ref[0, :, hd * LANES:(hd + 1) * LANES]
            s = _dot_t(q, k)
            m = jnp.max(s, axis=-1, keepdims=True)
            p = jnp.exp2((s - m).astype(BF16))
            outs.append(_dot(p, vaug_s[hd]))
        num = jnp.where(lo, outs[0], outs[1])
        den = jnp.where(lo, pltpu.roll(outs[0], HEAD_DIM, 1), pltpu.roll(outs[1], HEAD_DIM, 1))
        o_ref[0, rows, :] = (num * (1.0 / den)).astype(BF16)


def _mla_attention(q, k, v):
    b, t, _ = q.shape
    tq = min(MLA_Q_TILE, t)
    return pl.pallas_call(
        _mla_kernel,
        grid=(b, MLA_HEADS // 2, t // tq),
        in_specs=[
            pl.BlockSpec((1, tq, 2 * LANES), lambda i, c, j: (i, j, c)),
            pl.BlockSpec((1, t, 2 * LANES), lambda i, c, j: (i, 0, c)),
            pl.BlockSpec((1, t, LANES), lambda i, c, j: (i, 0, c)),
        ],
        out_specs=pl.BlockSpec((1, tq, LANES), lambda i, c, j: (i, j, c)),
        out_shape=jax.ShapeDtypeStruct((b, t, MLA_HEADS * MLA_V), BF16),
        scratch_shapes=[pltpu.VMEM((2, t, LANES), BF16)],
        compiler_params=_cparams("parallel", "parallel", "arbitrary"),
        name="mla_attn",
    )(q, k, v)


def _band_chain(qs_list, kw, vw, mask, tq):
    lo = _lane_lo()
    m_lo = jnp.where(lo, 1.0, 0.0).astype(BF16)
    m_hi = jnp.where(lo, 0.0, 1.0).astype(BF16)
    parts = []
    for qs in qs_list:
        parts += [qs * m_lo, qs * m_hi]
    s = mask(_dot_t(jnp.concatenate(parts, axis=0), kw))
    m = jnp.max(s, axis=-1, keepdims=True)
    p = jnp.exp2(s - m)
    l = jnp.sum(p, axis=-1, keepdims=True)
    o = _dot(p.astype(BF16), vw) * (1.0 / l)
    lse = (m + jnp.log2(l)) * math.log(2.0)
    res = []
    for qi in range(len(qs_list)):
        a, b = 2 * qi * tq, (2 * qi + 1) * tq
        res.append((jnp.where(lo, o[a:a + tq], o[b:b + tq]),
                    jnp.where(lo, lse[a:a + tq], lse[b:b + tq])))
    return res


def _band_rel(rows, tq, win, half):
    return (lax.rem(lax.broadcasted_iota(jnp.int32, (rows, win), 0), tq)
            - lax.broadcasted_iota(jnp.int32, (rows, win), 1)) + half


def _band_valid(relh, off, half):
    return lax.bitcast_convert_type(relh + off, jnp.uint32) <= jnp.uint32(2 * half)


def _win_kernel(q_ref, k_ref, v_ref, sink_ref, o_ref, *, seq, tq, win):
    half, nq = WIN_HALF, 2
    relh = _band_rel(2 * nq * tq, tq, win, half)

    def tile(ti, carry):
        q0 = pl.multiple_of(ti * tq, tq)
        ws = 0 if win == seq else pl.multiple_of(jnp.clip(q0 - half, 0, seq - win), HEAD_DIM)
        valid = _band_valid(relh, q0 - ws, half)
        qs = [q_ref[0, pl.ds(q0, tq), qi * LANES:(qi + 1) * LANES] for qi in range(nq)]
        res = _band_chain(qs, k_ref[0, pl.ds(ws, win), :], v_ref[0, pl.ds(ws, win), :],
                          lambda s: jnp.where(valid, s, NEG_INF), tq)
        for qi, (o_slab, lse_slab) in enumerate(res):
            keep = _sigmoid(lse_slab - sink_ref[:, qi * LANES:(qi + 1) * LANES])
            o_ref[0, pl.ds(q0, tq), qi * LANES:(qi + 1) * LANES] = (o_slab * keep).astype(BF16)
        return carry

    n_t = seq // tq
    lax.fori_loop(0, n_t, tile, 0, unroll=min(BAND_UNROLL, n_t))


def _win_attention(q, k, v, sink):
    b, seq, _ = q.shape
    tq = min(BAND_Q_TILE, seq)
    win = min(tq + 2 * WIN_HALF, seq)
    assert tq >= WIN_HALF or win == seq
    kern = functools.partial(_win_kernel, seq=seq, tq=tq, win=win)
    return pl.pallas_call(
        kern,
        grid=(b, WIN_KV_HEADS),
        in_specs=[
            pl.BlockSpec((1, seq, 2 * LANES), lambda i, c: (i, 0, c)),
            pl.BlockSpec((1, seq, LANES), lambda i, c: (i, 0, c)),
            pl.BlockSpec((1, seq, LANES), lambda i, c: (i, 0, c)),
            pl.BlockSpec((1, 2 * LANES), lambda i, c: (0, c)),
        ],
        out_specs=pl.BlockSpec((1, seq, 2 * LANES), lambda i, c: (i, 0, c)),
        out_shape=jax.ShapeDtypeStruct(q.shape, BF16),
        compiler_params=_cparams("parallel", "parallel"),
        name="win_attn",
    )(q, k, v, sink)


def _dil_kernel(q_ref, k_ref, v_ref, o_ref, qf, kf, vf, qd, kd, vd, acc, lse, of, *, seq):
    qf[...] = q_ref[0].astype(F32)
    kf[...] = k_ref[0].astype(F32)
    vf[...] = v_ref[0].astype(F32)
    for pi in (1, 2):
        d = DIL_PATTERNS[pi][1]
        ln = seq // d
        for r in range(d):
            for src, dst in ((qf, qd), (kf, kd), (vf, vd)):
                dst[pi - 1, r * ln:(r + 1) * ln, :] = src[pl.ds(r, ln, stride=d), :].astype(BF16)

    for pi, (window, d) in enumerate(DIL_PATTERNS):
        ln = seq // d
        half = window // (2 * d)
        tq = min(BAND_Q_TILE, ln)
        win = min(tq + 2 * half, ln)
        n_t = ln // tq
        align = min(tq, HEAD_DIM)
        rel = _band_rel(2 * tq, tq, win, half)
        if pi == 0:
            load = lambda ref, row, n: ref[0, pl.ds(row, n), :]
            srcs = (q_ref, k_ref, v_ref)
        else:
            load = lambda ref, row, n, pi=pi: ref[pi - 1, pl.ds(row, n), :]
            srcs = (qd, kd, vd)

        def body(i, carry, pi=pi, ln=ln, half=half, tq=tq, win=win, n_t=n_t, align=align,
                 rel=rel, load=load, srcs=srcs):
            r = i // n_t
            q0 = (i - r * n_t) * tq
            ws = 0 if win == ln else jnp.clip(q0 - half, 0, ln - win)
            valid = _band_valid(rel, q0 - ws, half)
            rq = pl.multiple_of(r * ln + q0, align)
            rk = pl.multiple_of(r * ln + ws, align)
            ((o_slab, lse_slab),) = _band_chain(
                [load(srcs[0], rq, tq)], load(srcs[1], rk, win), load(srcs[2], rk, win),
                lambda s: jnp.where(valid, s, NEG_INF), tq)
            acc[pi, pl.ds(rq, tq), :] = o_slab
            lse[pi, pl.ds(rq, tq), :] = lse_slab
            return carry

        lax.fori_loop(0, seq // tq, body, 0, unroll=min(16, seq // tq))

    d2, d3 = DIL_PATTERNS[1][1], DIL_PATTERNS[2][1]
    l2n, l3n = seq // d2, seq // d3
    for r in range(d3):
        s1 = pl.ds(r, l3n, stride=d3)
        s2 = pl.ds((r % d2) * l2n + r // d2, l3n, stride=d3 // d2)
        s3 = pl.ds(r * l3n, l3n)
        l1, l2, l3 = lse[0, s1, :], lse[1, s2, :], lse[2, s3, :]
        m = jnp.maximum(jnp.maximum(l1, l2), l3)
        e1, e2, e3 = jnp.exp(l1 - m), jnp.exp(l2 - m), jnp.exp(l3 - m)
        of[s1, :] = ((e1 * acc[0, s1, :] + e2 * acc[1, s2, :] + e3 * acc[2, s3, :])
                     * (1.0 / (e1 + e2 + e3)))
    o_ref[0] = of[...].astype(BF16)


def _dil_attention(q, k, v):
    b, seq, c = q.shape
    blk = pl.BlockSpec((1, seq, LANES), lambda i, j: (i, 0, j))
    f32s = pltpu.VMEM((seq, LANES), F32)
    res = pltpu.VMEM((2, seq, LANES), BF16)
    pat = pltpu.VMEM((len(DIL_PATTERNS), seq, LANES), F32)
    return pl.pallas_call(
        functools.partial(_dil_kernel, seq=seq),
        grid=(b, c // LANES),
        in_specs=[blk, blk, blk],
        out_specs=blk,
        out_shape=jax.ShapeDtypeStruct(q.shape, BF16),
        scratch_shapes=[f32s, f32s, f32s, res, res, res, pat, pat, f32s],
        compiler_params=_cparams("parallel", "parallel"),
        name="dil_attn",
    )(q, k, v)


def _even_out_kernel(x_ref, mod_ref, oa_ref, ob_ref, w_ref, y_ref):
    half = w_ref.shape[0] // 2
    mix = _dot(oa_ref[0], w_ref[:half, :]) + _dot(ob_ref[0], w_ref[half:, :])
    y_ref[0] = x_ref[0] + mod_ref[0, 2:3, :] * mix


def _even_out(x, mod_l, oa, ob, w_out):
    b, t, _ = x.shape
    tm = min(STREAM_ROW_TILE, t)
    row = lambda c: pl.BlockSpec((1, tm, c), lambda i, j: (i, j, 0))
    return pl.pallas_call(
        _even_out_kernel,
        grid=(b, t // tm),
        in_specs=[row(D_MODEL), pl.BlockSpec((1, 6, D_MODEL), lambda i, j: (i, 0, 0)),
                  row(oa.shape[2]), row(ob.shape[2]), _const_spec(w_out.shape)],
        out_specs=row(D_MODEL),
        out_shape=jax.ShapeDtypeStruct(x.shape, F32),
        compiler_params=_cparams("parallel", "parallel"),
        name="even_out",
    )(x, mod_l, oa, ob, w_out)


def _dense_ffn_kernel(x_ref, mod_ref, ng_ref, wu_ref, wd_ref, y_ref):
    x = x_ref[0]
    h = _norm_mod(x, ng_ref[...], mod_ref[0, 4:5, :], mod_ref[0, 3:4, :]).astype(BF16)
    ff = wd_ref.shape[0]
    acc = None
    c0 = 0
    for cw in DENSE_FF_CHUNKS:
        g = _dot(h, wu_ref[:, c0:c0 + cw])
        u = _dot(h, wu_ref[:, ff + c0:ff + c0 + cw])
        a = (g * _sigmoid(g) * u).astype(BF16)
        d = _dot(a, wd_ref[c0:c0 + cw, :])
        acc = d if acc is None else acc + d
        c0 += cw
    y_ref[0] = x + mod_ref[0, 5:6, :] * acc


def _dense_ffn(x, mod_l, ng, w_up, w_down):
    b, t, _ = x.shape
    tm = min(ROW_TILE, t)
    row = pl.BlockSpec((1, tm, D_MODEL), lambda i, j: (i, j, 0))
    return pl.pallas_call(
        _dense_ffn_kernel,
        grid=(b, t // tm),
        in_specs=[row, pl.BlockSpec((1, 6, D_MODEL), lambda i, j: (i, 0, 0)),
                  _const_spec(ng.shape), _const_spec(w_up.shape), _const_spec(w_down.shape)],
        out_specs=row,
        out_shape=jax.ShapeDtypeStruct(x.shape, F32),
        compiler_params=_cparams("parallel", "parallel"),
        name="dense_ffn",
    )(x, mod_l, ng, w_up, w_down)


def _odd_in_kernel(x_ref, mod_ref, ng_ref, w_in_ref, tq_ref, tk_ref, q_ref, k_ref, v_ref):
    h = _norm_mod(x_ref[0], ng_ref[...], mod_ref[0, 1:2, :], mod_ref[0, 0:1, :]).astype(BF16)
    proj = _dot(h, w_in_ref[...])
    tq = (tq_ref[0], tq_ref[1])
    tk = (tk_ref[0], tk_ref[1])
    nq = WIN_Q_HEADS * HEAD_DIM // LANES
    part = (nq + 2 * WIN_KV_HEADS) * LANES
    for j in range(nq):
        q_ref[0, :, j * LANES:(j + 1) * LANES] = _norm_rope_pair(
            proj[:, j * LANES:(j + 1) * LANES],
            proj[:, part + j * LANES:part + (j + 1) * LANES], tq).astype(BF16)
    for j in range(WIN_KV_HEADS):
        c0 = (nq + j) * LANES
        k_ref[0, :, j * LANES:(j + 1) * LANES] = _norm_rope_pair(
            proj[:, c0:c0 + LANES], proj[:, part + c0:part + c0 + LANES], tk).astype(BF16)
    v_ref[0] = proj[:, (nq + WIN_KV_HEADS) * LANES:part].astype(BF16)


def _odd_in(x, mod_l, ng, w_in, tabs):
    b, t, _ = x.shape
    tm = min(ROW_TILE, t)
    row = lambda c: pl.BlockSpec((1, tm, c), lambda i, j: (i, j, 0))
    tab = pl.BlockSpec((2, tm, LANES), lambda i, j: (0, j, 0))
    widths = (WIN_Q_HEADS * HEAD_DIM, WIN_KV_HEADS * LANES, WIN_KV_HEADS * LANES)
    return pl.pallas_call(
        _odd_in_kernel,
        grid=(b, t // tm),
        in_specs=[row(D_MODEL), pl.BlockSpec((1, 6, D_MODEL), lambda i, j: (i, 0, 0)),
                  _const_spec(ng.shape), _const_spec(w_in.shape), tab, tab],
        out_specs=[row(c) for c in widths],
        out_shape=[jax.ShapeDtypeStruct((b, t, c), BF16) for c in widths],
        compiler_params=_cparams("parallel", "parallel"),
        name="odd_in",
    )(x, mod_l, ng, w_in, *tabs)


def _odd_out_kernel(x_ref, mod_ref, o_ref, w_ref, y_ref):
    y_ref[0] = x_ref[0] + mod_ref[0, 2:3, :] * _dot(o_ref[0], w_ref[...])


def _odd_out(x, mod_l, o, w_out):
    b, t, _ = x.shape
    tm = min(STREAM_ROW_TILE, t)
    row = pl.BlockSpec((1, tm, D_MODEL), lambda i, j: (i, j, 0))
    return pl.pallas_call(
        _odd_out_kernel,
        grid=(b, t // tm),
        in_specs=[row, pl.BlockSpec((1, 6, D_MODEL), lambda i, j: (i, 0, 0)), row,
                  _const_spec(w_out.shape)],
        out_specs=row,
        out_shape=jax.ShapeDtypeStruct(x.shape, F32),
        compiler_params=_cparams("parallel", "parallel"),
        name="odd_out",
    )(x, mod_l, o, w_out)


def _router_kernel(x_ref, mod_ref, ng_ref, wr_ref, h_ref, gate_ref, sel_ref):
    h = _norm_mod(x_ref[0], ng_ref[...], mod_ref[0, 4:5, :], mod_ref[0, 3:4, :])
    w = _pack_row(h)
    h_ref[0, 0] = w[:, :PLANE_W]
    h_ref[1, 0] = w[:, PLANE_W:]
    lane = lax.broadcasted_iota(jnp.int32, (1, LANES), 1).astype(F32)
    logits = jnp.where(lane < N_EXPERTS, _dot3(h, wr_ref[...]), NEG_INF)
    e = jnp.exp(logits - jnp.max(logits, axis=-1, keepdims=True))
    p = e * (1.0 / jnp.sum(e, axis=-1, keepdims=True))
    p = jnp.where(lane < N_EXPERTS, p, -1.0)
    m1 = jnp.max(p, axis=-1, keepdims=True)
    i1 = jnp.min(jnp.where(p == m1, lane, float(LANES)), axis=-1, keepdims=True)
    s1 = lane == i1
    p2 = jnp.where(s1, -1.0, p)
    m2 = jnp.max(p2, axis=-1, keepdims=True)
    i2 = jnp.min(jnp.where(p2 == m2, lane, float(LANES)), axis=-1, keepdims=True)
    s2 = lane == i2
    inv = 1.0 / (m1 + m2)
    gate_ref[0] = jnp.where(s1, m1 * inv, 0.0) + jnp.where(s2, m2 * inv, 0.0)
    sel_ref[0] = jnp.where(s1, 1.0, 0.0) + jnp.where(s2, 1.0, 0.0)


def _router(x, mod_l, ng, wr):
    b, t, _ = x.shape
    tm = min(ROW_TILE, t)
    row = lambda c: pl.BlockSpec((1, tm, c), lambda i, j: (i, j, 0))
    return pl.pallas_call(
        _router_kernel,
        grid=(b, t // tm),
        in_specs=[row(D_MODEL), pl.BlockSpec((1, 6, D_MODEL), lambda i, j: (i, 0, 0)),
                  _const_spec(ng.shape), _const_spec(wr.shape)],
        out_specs=[pl.BlockSpec((2, 1, tm, PLANE_W), lambda i, j: (0, i, j, 0)), row(LANES), row(LANES)],
        out_shape=[jax.ShapeDtypeStruct((2, b, t, PLANE_W), jnp.uint32),
                   jax.ShapeDtypeStruct((b, t, LANES), F32),
                   jax.ShapeDtypeStruct((b, t, LANES), F32)],
        compiler_params=_cparams("parallel", "parallel"),
        name="moe_router",
    )(x, mod_l, ng, wr)


SC_WINDOW = 128
SC_COLS = PLANE_W


def _sc_mesh():
    return plsc.VectorSubcoreMesh(core_axis_name="core", subcore_axis_name="subcore")


def _scatter_rows_sc(src, pos_a, pos_b, n_out):
    n = src.shape[0]

    @pl.kernel(out_type=jax.ShapeDtypeStruct((n_out, SC_COLS), src.dtype), mesh=_sc_mesh(),
               scratch_types=[])
    def scatter(src_hbm, ia_hbm, ib_hbm, dst_hbm):
        def body(rows_vmem, ia_vmem, ib_vmem):
            pltpu.sync_copy(rows_vmem, dst_hbm.at[ia_vmem.at[0]])
            pltpu.sync_copy(rows_vmem, dst_hbm.at[ib_vmem.at[0]])

        idx_spec = pl.BlockSpec((1, SC_WINDOW), lambda i: (0, i))
        pltpu.emit_pipeline(
            body,
            grid=(n // SC_WINDOW,),
            in_specs=[pl.BlockSpec((SC_WINDOW, SC_COLS), lambda i: (i, 0)), idx_spec, idx_spec],
            out_specs=[],
            core_axis_name=("core", "subcore"),
            dimension_semantics=(pltpu.PARALLEL,),
        )(src_hbm, ia_hbm, ib_hbm)

    return scatter(src, pos_a.reshape(1, n), pos_b.reshape(1, n))


def _gather_rows_sc(src, pos):
    n = pos.shape[0]

    @pl.kernel(out_type=jax.ShapeDtypeStruct((n, SC_COLS), src.dtype), mesh=_sc_mesh(),
               scratch_types=[])
    def gather(src_hbm, i_hbm, dst_hbm):
        def body(i_vmem, rows_vmem):
            pltpu.sync_copy(src_hbm.at[i_vmem.at[0]], rows_vmem)

        pltpu.emit_pipeline(
            body,
            grid=(n // SC_WINDOW,),
            in_specs=[pl.BlockSpec((1, SC_WINDOW), lambda i: (0, i))],
            out_specs=[pl.BlockSpec((SC_WINDOW, SC_COLS), lambda i: (i, 0))],
            core_axis_name=("core", "subcore"),
            dimension_semantics=(pltpu.PARALLEL,),
        )(i_hbm, dst_hbm)

    return gather(src, pos.reshape(1, n))


def _expert_kernel(te_ref, nu_ref, x0_ref, x1_ref, wg_ref, wu_ref, wd_ref, ys_ref, xb_ref, acc_ref):
    i, j = pl.program_id(0), pl.program_id(1)
    used = i < nu_ref[0]

    @pl.when(jnp.logical_and(used, j == 0))
    def _():
        for pi, x_ref in enumerate((x0_ref, x1_ref)):
            lo, hi = _unpack_row(x_ref[0])
            xb_ref[:, pi * PLANE_W:(pi + 1) * PLANE_W] = lo.astype(BF16)
            xb_ref[:, HALF_D + pi * PLANE_W:HALF_D + (pi + 1) * PLANE_W] = hi.astype(BF16)
        acc_ref[...] = jnp.zeros_like(acc_ref)

    @pl.when(used)
    def _():
        xb = xb_ref[...]
        g = _dot(xb, wg_ref[0])
        u = _dot(xb, wu_ref[0])
        a = (g * _sigmoid(g) * u).astype(BF16)
        acc_ref[...] += _dot(a, wd_ref[0])

    last = j == pl.num_programs(1) - 1

    @pl.when(jnp.logical_and(used, last))
    def _():
        w = _pack_row(acc_ref[...])
        ys_ref[0] = w[:, :PLANE_W]
        ys_ref[1] = w[:, PLANE_W:]

    @pl.when(jnp.logical_and(jnp.logical_not(used), last))
    def _():
        ys_ref[...] = jnp.zeros_like(ys_ref)


def _experts(xs, tile_expert, n_used, w_up, w_down):
    p = xs.shape[1]
    nck = FF_EXPERT // MOE_FF_CHUNK
    tm = MOE_ROW_TILE

    def chunk(i, j, te, nu):
        return jnp.where(i < nu[0], j, nck - 1)

    grid_spec = pltpu.PrefetchScalarGridSpec(
        num_scalar_prefetch=2,
        grid=(p // tm, nck),
        in_specs=[
            pl.BlockSpec((1, tm, PLANE_W), lambda i, j, te, nu: (0, i, 0)),
            pl.BlockSpec((1, tm, PLANE_W), lambda i, j, te, nu: (1, i, 0)),
            pl.BlockSpec((1, D_MODEL, MOE_FF_CHUNK), lambda i, j, te, nu: (te[i], 0, chunk(i, j, te, nu))),
            pl.BlockSpec((1, D_MODEL, MOE_FF_CHUNK),
                         lambda i, j, te, nu: (te[i], 0, nck + chunk(i, j, te, nu))),
            pl.BlockSpec((1, MOE_FF_CHUNK, D_MODEL), lambda i, j, te, nu: (te[i], chunk(i, j, te, nu), 0)),
        ],
        out_specs=pl.BlockSpec((2, tm, PLANE_W), lambda i, j, te, nu: (0, i, 0)),
        scratch_shapes=[pltpu.VMEM((tm, D_MODEL), BF16), pltpu.VMEM((tm, D_MODEL), F32)],
    )
    return pl.pallas_call(
        _expert_kernel,
        grid_spec=grid_spec,
        out_shape=jax.ShapeDtypeStruct((2, p, PLANE_W), jnp.uint32),
        compiler_params=_cparams("arbitrary", "arbitrary"),
        name="moe_experts",
    )(tile_expert, n_used, xs, xs, w_up, w_up, w_down)


def _combine_kernel(x_ref, mod_ref, g_ref, a0_ref, b0_ref, a1_ref, b1_ref, o_ref):
    g = g_ref[0]
    ga, gb = g[:, 0:1], g[:, 1:2]
    for pi, (a_ref, b_ref) in enumerate(((a0_ref, b0_ref), (a1_ref, b1_ref))):
        a_lo, a_hi = _unpack_row(a_ref[0, 0, 0])
        b_lo, b_hi = _unpack_row(b_ref[0, 0, 0])
        for c0, ff in ((pi * PLANE_W, ga * a_lo + gb * b_lo),
                       (HALF_D + pi * PLANE_W, ga * a_hi + gb * b_hi)):
            cols = slice(c0, c0 + PLANE_W)
            o_ref[0, :, cols] = x_ref[0, :, cols] + mod_ref[0, 5:6, cols] * ff


def _combine(x, mod_l, gate_ab, yg):
    b, t, _ = x.shape
    tm = min(STREAM_ROW_TILE, t)
    row = pl.BlockSpec((1, tm, D_MODEL), lambda i, j: (i, j, 0))
    part = lambda pi, ab: pl.BlockSpec((1, 1, 1, tm, PLANE_W), lambda i, j: (pi, ab, i, j, 0))
    return pl.pallas_call(
        _combine_kernel,
        grid=(b, t // tm),
        in_specs=[row, pl.BlockSpec((1, 6, D_MODEL), lambda i, j: (i, 0, 0)),
                  pl.BlockSpec((1, tm, 2), lambda i, j: (i, j, 0)),
                  part(0, 0), part(0, 1), part(1, 0), part(1, 1)],
        out_specs=row,
        out_shape=jax.ShapeDtypeStruct(x.shape, F32),
        compiler_params=_cparams("parallel", "parallel"),
        name="moe_combine",
    )(x, mod_l, gate_ab, yg, yg, yg, yg)


def _moe(x, mod_l, ng, wr, w_up, w_down):
    b, t, _ = x.shape
    n = b * t
    h, gates, sel = _router(x, mod_l, ng, wr)
    gates = gates.reshape(n, LANES)[:, :N_EXPERTS]
    sel = sel.reshape(n, LANES)[:, :N_EXPERTS] > 0.5

    tm = MOE_ROW_TILE
    n_tiles = 2 * n // tm + N_EXPERTS
    p = n_tiles * tm
    sel_i = sel.astype(jnp.int32)
    rank = jnp.cumsum(sel_i, axis=0) - sel_i
    counts = jnp.sum(sel_i, axis=0)
    padded = (counts + tm - 1) // tm * tm
    ends = jnp.cumsum(padded)
    pos = (ends - padded)[None, :] + rank
    tile_start = jnp.arange(n_tiles, dtype=jnp.int32) * tm
    tile_expert = jnp.minimum(jnp.sum((tile_start[:, None] >= ends[None, :]).astype(jnp.int32), axis=1),
                              N_EXPERTS - 1)
    n_used = (ends[-1] // tm).astype(jnp.int32).reshape(1)
    order = jnp.cumsum(sel_i, axis=1)
    first, second = sel & (order == 1), sel & (order == 2)
    pos_a = jnp.sum(jnp.where(first, pos, 0), axis=1)
    pos_b = jnp.sum(jnp.where(second, pos, 0), axis=1)
    gate_ab = jnp.stack([jnp.sum(jnp.where(first, gates, 0.0), axis=1),
                         jnp.sum(jnp.where(second, gates, 0.0), axis=1)], axis=1)

    xs = _scatter_rows_sc(h.reshape(2 * n, PLANE_W), jnp.concatenate([pos_a, pos_a + p]),
                          jnp.concatenate([pos_b, pos_b + p]), 2 * p)
    ys = _experts(xs.reshape(2, p, PLANE_W), tile_expert, n_used, w_up, w_down)
    yg = _gather_rows_sc(ys.reshape(2 * p, PLANE_W),
                         jnp.concatenate([pos_a, pos_b, pos_a + p, pos_b + p]))
    return _combine(x, mod_l, gate_ab.reshape(b, t, 2), yg.reshape(2, 2, b, t, PLANE_W))


def _rope_tables(t, dim):
    inv = ROPE_THETA ** (-jnp.arange(0, dim, 2, dtype=F32) / dim)
    ang = jnp.arange(t, dtype=F32)[:, None] * inv[None, :]
    return jnp.cos(ang), jnp.sin(ang)


def _rope_lane_tables(cos, sin, gain, offs, scale):
    t, hw = cos.shape
    c, s = jnp.ones((t, LANES), F32), jnp.zeros((t, LANES), F32)
    g_up, g_dn = jnp.roll(gain, hw), jnp.roll(gain, -hw)
    g_partner = jnp.zeros((LANES,), F32)
    for o in offs:
        c = c.at[:, o:o + hw].set(cos).at[:, o + hw:o + 2 * hw].set(cos)
        s = s.at[:, o:o + hw].set(-sin).at[:, o + hw:o + 2 * hw].set(sin)
        g_partner = g_partner.at[o:o + hw].set(g_dn[o:o + hw]).at[o + hw:o + 2 * hw].set(
            g_up[o + hw:o + 2 * hw])
    return jnp.stack([c * gain, s * g_partner]) * scale


def _swap_halves(w, width):
    d, c = w.shape
    return w.reshape(d, c // width, 2, width // 2)[:, :, ::-1, :].reshape(d, c)


def _pad_lanes(v, width=LANES):
    return jnp.zeros((width,), F32).at[:v.shape[0]].set(v)


def _even_weights(w_in, qlg, q_up, kvlg, kv_up):
    w_kr = w_in[:, KR_OFF:KR_OFF + MLA_ROPE]
    slab = lambda w: jnp.zeros((D_MODEL, LANES), F32).at[:, MLA_NOPE:MLA_QK].set(w)
    w_dil = w_in[:, KR_OFF + MLA_ROPE:]
    n_qk = 2 * DIL_HEADS * HEAD_DIM
    w_in_p = jnp.concatenate([w_in[:, :KR_OFF], slab(w_kr), slab(_swap_halves(w_kr, MLA_ROPE)),
                              w_dil, _swap_halves(w_dil[:, :n_qk], HEAD_DIM)], axis=1)
    q3 = q_up.reshape(MLA_Q_RANK, MLA_HEADS, MLA_QK)
    q_rope_p = _swap_halves(q3[:, :, MLA_NOPE:].reshape(MLA_Q_RANK, -1), MLA_ROPE)
    q_part = jnp.pad(q_rope_p.reshape(MLA_Q_RANK, MLA_HEADS, MLA_ROPE),
                     ((0, 0), (0, 0), (MLA_NOPE, LANES - MLA_QK)))
    q_up_p = jnp.concatenate([jnp.pad(q3, ((0, 0), (0, 0), (0, LANES - MLA_QK))), q_part],
                             axis=1).reshape(MLA_Q_RANK, 2 * MLA_HEADS * LANES)
    kv3 = kv_up.reshape(MLA_KV_RANK, MLA_HEADS, MLA_NOPE + MLA_V)
    k_p = jnp.pad(kv3[:, :, :MLA_NOPE], ((0, 0), (0, 0), (0, LANES - MLA_NOPE)))
    kv_up_p = jnp.concatenate([k_p.reshape(MLA_KV_RANK, -1),
                               kv3[:, :, MLA_NOPE:].reshape(MLA_KV_RANK, -1)], axis=1)
    return (w_in_p.astype(BF16), qlg.reshape(1, -1), q_up_p.astype(BF16), kvlg.reshape(1, -1),
            kv_up_p.astype(BF16))


def _odd_weights(w_in):
    nq = WIN_Q_HEADS * HEAD_DIM
    nkv = WIN_KV_HEADS * HEAD_DIM
    dup = lambda w: jnp.concatenate([w.reshape(D_MODEL, WIN_KV_HEADS, 1, HEAD_DIM)] * 2,
                                    axis=2).reshape(D_MODEL, 2 * nkv)
    w_q, w_k = w_in[:, :nq], dup(w_in[:, nq:nq + nkv])
    return jnp.concatenate([w_q, w_k, dup(w_in[:, nq + nkv:]), _swap_halves(w_q, HEAD_DIM),
                            _swap_halves(w_k, HEAD_DIM)], axis=1).astype(BF16)


def _trunk(x, c, p):
    b, t, _ = x.shape
    mod = _ada_mod(c, p["ada_w"], p["ada_b"])
    cos_r, sin_r = _rope_tables(t, MLA_ROPE)
    cos_h, sin_h = _rope_tables(t, HEAD_DIM)
    pair_offs = (0, HEAD_DIM)
    for l in range(DEPTH):
        i = l // 2
        mod_l = mod[l]
        ng = p["norm_g"][l]
        if l % 2 == 0:
            g_q, g_k = p["even_mla_qk_g"][i, 0], p["even_mla_qk_g"][i, 1]
            g_qb, g_kb = p["even_dil_qk_g"][i, 0], p["even_dil_qk_g"][i, 1]
            tabs = (
                _rope_lane_tables(cos_r, sin_r, _pad_lanes(g_q), (MLA_NOPE,),
                                  MLA_QK ** -0.5 * math.log2(math.e)),
                _rope_lane_tables(cos_r, sin_r, _pad_lanes(g_k).at[:MLA_NOPE].set(0.0), (MLA_NOPE,), 1.0),
                _rope_lane_tables(cos_h, sin_h, jnp.tile(g_qb, 2), pair_offs,
                                  HEAD_DIM ** -0.5 * math.log2(math.e)),
                _rope_lane_tables(cos_h, sin_h, jnp.tile(g_kb, 2), pair_offs, 1.0),
            )
            gkn = _pad_lanes(g_k[:MLA_NOPE]).reshape(1, LANES)
            wts = _even_weights(p["even_w_in"][i], p["even_q_latent_g"][i], p["even_q_up"][i],
                                p["even_kv_latent_g"][i], p["even_kv_up"][i]) + (gkn,)
            qa, ka, va, qb, kb, vb = _even_in(x, mod_l, ng[0:1], wts, tabs)
            oa = _mla_attention(qa, ka, va)
            ob = _dil_attention(qb, kb, vb)
            x = _even_out(x, mod_l, oa, ob, p["even_w_out"][i].astype(BF16))
            x = _dense_ffn(x, mod_l, ng[1:2], p["dense_w_up"][i].astype(BF16),
                           p["dense_w_down"][i].astype(BF16))
        else:
            g_q, g_k = p["odd_qk_g"][i, 0], p["odd_qk_g"][i, 1]
            tabs = (
                _rope_lane_tables(cos_h, sin_h, jnp.tile(g_q, 2), pair_offs,
                                  HEAD_DIM ** -0.5 * math.log2(math.e)),
                _rope_lane_tables(cos_h, sin_h, jnp.tile(g_k, 2), pair_offs, 1.0),
            )
            q, k, v = _odd_in(x, mod_l, ng[0:1], _odd_weights(p["odd_w_in"][i]), tabs)
            sink = jnp.repeat(p["odd_sink"][i], HEAD_DIM).reshape(1, -1)
            o = _win_attention(q, k, v, sink)
            x = _odd_out(x, mod_l, o, p["odd_w_out"][i].astype(BF16))
            wr = jnp.zeros((D_MODEL, LANES), F32).at[:, :N_EXPERTS].set(p["moe_router"][i])
            x = _moe(x, mod_l, ng[1:2], wr, p["moe_w_up"][i].astype(BF16),
                     p["moe_w_down"][i].astype(BF16))
    return x


def kernel(x_prompt, x_sample, c_prompt, c_sample, ada_w, ada_b, norm_g, even_w_in, even_q_latent_g, even_q_up, even_kv_latent_g, even_kv_up, even_mla_qk_g, even_dil_qk_g, even_w_out, odd_w_in, odd_qk_g, odd_sink, odd_w_out, dense_w_up, dense_w_down, moe_router, moe_w_up, moe_w_down):
    p = dict(ada_w=ada_w, ada_b=ada_b, norm_g=norm_g, even_w_in=even_w_in,
             even_q_latent_g=even_q_latent_g, even_q_up=even_q_up,
             even_kv_latent_g=even_kv_latent_g, even_kv_up=even_kv_up,
             even_mla_qk_g=even_mla_qk_g, even_dil_qk_g=even_dil_qk_g, even_w_out=even_w_out,
             odd_w_in=odd_w_in, odd_qk_g=odd_qk_g, odd_sink=odd_sink, odd_w_out=odd_w_out,
             dense_w_up=dense_w_up, dense_w_down=dense_w_down, moe_router=moe_router,
             moe_w_up=moe_w_up, moe_w_down=moe_w_down)
    return _trunk(x_prompt, c_prompt, p), _trunk(x_sample, c_sample, p)
```
